```python
import numpy as np
import jax
import jax.numpy as jnp
from jax import lax

D_MODEL = 1024
BATCH = 16
SEQ = 4096
DEPTH = 2
DEC_BATCH = 16
DEC_SEQ = 2048
PAST_LEN = 128

GRID_W = 64
RET_HEADS = 4
RET_QK_DIM = 128
RET_V_DIM = 256
RET_CHUNK = 128
RET_QK_WIDTH = RET_HEADS * RET_QK_DIM
RET_V_WIDTH = RET_HEADS * RET_V_DIM
NA_HEADS = 8
NA_HEAD_DIM = 64
NA_WIDTH = NA_HEADS * NA_HEAD_DIM
NA_KH_MAX = 8
NA_KW = 16
NA_Q_COL_BLOCK = 16
NA_K_COL_BLOCK = NA_Q_COL_BLOCK + NA_KW
D_FF = 2816
N_MOD = 9
ROPE_BASE = 10000.0
RMS_EPS = 1e-6
GN_EPS = 1e-5
IN_SPLITS = (RET_QK_WIDTH, RET_QK_WIDTH, RET_V_WIDTH, RET_V_WIDTH,
             NA_WIDTH, NA_WIDTH, NA_WIDTH, D_MODEL, D_MODEL)
IN_WIDTH = 2 * RET_QK_WIDTH + 2 * RET_V_WIDTH + 3 * NA_WIDTH + 2 * D_MODEL

kernel_name = 'hybrid_retention_natten_macaron_encoder'


def rms_norm(x, gain):
    xf = x.astype(jnp.float32)
    y = xf * lax.rsqrt(jnp.mean(xf * xf, axis=-1, keepdims=True) + RMS_EPS)
    return (y * gain.astype(jnp.float32)).astype(x.dtype)


def modulate(x, shift, scale):
    return x * (1.0 + scale) + shift


def swiglu(x, wi, wo):
    a, b = jnp.split(x @ wi, 2, axis=-1)
    return (jax.nn.silu(a) * b) @ wo


def rotary(x, pos):
    d = x.shape[-1]
    half = d // 2
    inv = ROPE_BASE ** (-jnp.arange(half, dtype=jnp.float32) / half)
    ang = pos[:, None] * inv[None, :]
    cos = jnp.cos(ang)[None, :, None, :]
    sin = jnp.sin(ang)[None, :, None, :]
    xf = x.astype(jnp.float32)
    x1, x2 = xf[..., :half], xf[..., half:]
    return jnp.concatenate([x1 * cos - x2 * sin, x1 * sin + x2 * cos], axis=-1).astype(x.dtype)


def retention_direction(q, k, v, log_gamma, include_diag):
    B, T, H, dk = q.shape
    dv = v.shape[-1]
    C = RET_CHUNK
    N = T // C

    def to_chunks(a):
        return a.astype(jnp.float32).reshape(B, N, C, H, a.shape[-1]).transpose(1, 0, 3, 2, 4)

    qc, kc, vc = to_chunks(q), to_chunks(k), to_chunks(v)
    pos = jnp.arange(C, dtype=jnp.float32)
    diff = pos[:, None] - pos[None, :]
    mask = (diff >= 0) if include_diag else (diff > 0)
    lg = log_gamma[:, None, None]
    decay = jnp.where(mask[None], jnp.exp(jnp.where(mask, diff, 0.0)[None] * lg), 0.0)
    xi = jnp.exp((pos[None, :] + 1.0) * log_gamma[:, None])[None, :, :, None]
    zeta = jnp.exp((C - 1.0 - pos[None, :]) * log_gamma[:, None])[None, :, :, None]
    g_chunk = jnp.exp(C * log_gamma)[None, :, None, None]

    def step(state, xs):
        q_i, k_i, v_i = xs
        s = jnp.einsum('bhnd,bhmd->bhnm', q_i, k_i) * decay[None]
        inner = jnp.einsum('bhnm,bhme->bhne', s, v_i)
        cross = jnp.einsum('bhnd,bhde->bhne', q_i, state) * xi
        new_state = state * g_chunk + jnp.einsum('bhmd,bhme->bhde', k_i * zeta, v_i)
        return new_state, inner + cross

    state0 = jnp.zeros((B, H, dk, dv), jnp.float32)
    _, out = lax.scan(step, state0, (qc, kc, vc))
    return out.transpose(1, 0, 3, 2, 4).reshape(B, T, H, dv)


def bidirectional_retention(q, k, v, decay_logit):
    log_g = jax.nn.log_sigmoid(decay_logit.astype(jnp.float32))
    fwd = retention_direction(q, k, v, log_g[0], True)
    bwd = jnp.flip(retention_direction(jnp.flip(q, 1), jnp.flip(k, 1), jnp.flip(v, 1), log_g[1], False), 1)
    return fwd + bwd


def neighborhood_attention(q, k, v, rpb):
    B, T, H, d = q.shape
    rows = T // GRID_W
    kh = min(NA_KH_MAX, rows)
    ncb = GRID_W // NA_Q_COL_BLOCK

    def to_grid(a):
        return a.reshape(B, rows, GRID_W, H, d).transpose(0, 3, 1, 2, 4)

    qg, kg, vg = to_grid(q), to_grid(k), to_grid(v)
    qcol = np.arange(GRID_W).reshape(ncb, NA_Q_COL_BLOCK)
    cstart = np.clip(qcol - NA_KW // 2, 0, GRID_W - NA_KW)
    kstart = np.clip(np.arange(ncb) * NA_Q_COL_BLOCK - NA_KW // 2, 0, GRID_W - NA_K_COL_BLOCK)
    kcol = kstart[:, None] + np.arange(NA_K_COL_BLOCK)[None, :]
    kc3 = kcol[:, None, :]
    colmask = jnp.asarray((kc3 >= cstart[:, :, None]) & (kc3 < cstart[:, :, None] + NA_KW))
    colidx = jnp.asarray(np.clip(kc3 - qcol[:, :, None] + NA_KW - 1, 0, 2 * NA_KW - 2))
    scale = d ** -0.5

    def one_row(r):
        start = jnp.clip(r - kh // 2, 0, rows - kh)
        q_r = lax.dynamic_index_in_dim(qg, r, axis=2, keepdims=False).reshape(B, H, ncb, NA_Q_COL_BLOCK, d)
        k_r = lax.dynamic_slice_in_dim(kg, start, kh, axis=2)[:, :, :, kcol]
        v_r = lax.dynamic_slice_in_dim(vg, start, kh, axis=2)[:, :, :, kcol]
        rowoff = start + jnp.arange(kh) - r + NA_KH_MAX - 1
        bias = rpb[:, rowoff[None, None, :, None], colidx[:, :, None, :]]
        s = jnp.einsum('bhnqd,bhrnkd->bhnqrk', q_r, k_r).astype(jnp.float32) * scale
        s = s + bias.astype(jnp.float32)[None]
        s = jnp.where(colmask[:, :, None, :], s, -jnp.inf)
        p = jax.nn.softmax(s.reshape(s.shape[:4] + (kh * NA_K_COL_BLOCK,)), axis=-1).reshape(s.shape)
        o = jnp.einsum('bhnqrk,bhrnkd->bhnqd', p.astype(v_r.dtype), v_r)
        return o.reshape(B, H, GRID_W, d)

    out = lax.map(one_row, jnp.arange(rows))
    return out.transpose(1, 0, 3, 2, 4).reshape(B, T, H * d)


def token_mixer(n, w_in, decay_logit, gn_gain, w_ret_o, rpb, w_na_o, w_out):
    B, T, _ = n.shape
    splits = [int(s) for s in np.cumsum(IN_SPLITS)[:-1]]
    q_r, k_r, v_r, g_r, q_n, k_n, v_n, gate_r, gate_n = jnp.split(n @ w_in, splits, axis=-1)
    pos = jnp.arange(T, dtype=jnp.float32)
    q_r = rotary(q_r.reshape(B, T, RET_HEADS, RET_QK_DIM), pos)
    k_r = rotary(k_r.reshape(B, T, RET_HEADS, RET_QK_DIM), pos) * (RET_QK_DIM ** -0.5)
    v_r = v_r.reshape(B, T, RET_HEADS, RET_V_DIM)
    y = bidirectional_retention(q_r, k_r, v_r, decay_logit)
    mu = jnp.mean(y, axis=-1, keepdims=True)
    var = jnp.mean(jnp.square(y - mu), axis=-1, keepdims=True)
    y = ((y - mu) * lax.rsqrt(var + GN_EPS)).reshape(B, T, RET_V_WIDTH) * gn_gain.astype(jnp.float32)
    ret = (jax.nn.silu(g_r) * y.astype(n.dtype)) @ w_ret_o
    na = neighborhood_attention(q_n.reshape(B, T, NA_HEADS, NA_HEAD_DIM),
                                k_n.reshape(B, T, NA_HEADS, NA_HEAD_DIM),
                                v_n.reshape(B, T, NA_HEADS, NA_HEAD_DIM), rpb) @ w_na_o
    merged = jax.nn.sigmoid(gate_r) * ret + jax.nn.sigmoid(gate_n) * na
    return merged @ w_out


def encoder_layer(h, c, ada_w, ada_b, ffn1_norm, ffn1_wi, ffn1_wo, mix_norm, w_in, ret_decay_logit,
                  ret_gn_gain, w_ret_o, na_rpb, w_na_o, w_out, ffn2_norm, ffn2_wi, ffn2_wo):
    mod = (jax.nn.silu(c) @ ada_w + ada_b)[:, None, :]
    sh1, sc1, g1, sh2, sc2, g2, sh3, sc3, g3 = jnp.split(mod, N_MOD, axis=-1)
    h = h + 0.5 * g1 * swiglu(modulate(rms_norm(h, ffn1_norm), sh1, sc1), ffn1_wi, ffn1_wo)
    h = h + g2 * token_mixer(modulate(rms_norm(h, mix_norm), sh2, sc2), w_in, ret_decay_logit,
                             ret_gn_gain, w_ret_o, na_rpb, w_na_o, w_out)
    h = h + 0.5 * g3 * swiglu(modulate(rms_norm(h, ffn2_norm), sh3, sc3), ffn2_wi, ffn2_wo)
    return h


def trunk(x, c, layer_weights, final_norm):
    h = x
    for l in range(DEPTH):
        h = encoder_layer(h, c, *[w[l] for w in layer_weights])
    return rms_norm(h, final_norm)


def setup_inputs(seed: int = 0) -> dict:
    key = jax.random.key(seed)
    ks = jax.random.split(key, 24)
    f32 = jnp.float32
    L, D = DEPTH, D_MODEL

    def nrm(k, shape, s):
        return jax.random.normal(k, shape, f32) * s

    sched = 1.0 - 2.0 ** (-5.0 - np.arange(RET_HEADS))
    logit = np.log(sched / (1.0 - sched))
    base = jnp.asarray(np.stack([logit, logit[::-1]]), dtype=f32)
    return {
        'x_prompt': nrm(ks[0], (BATCH, SEQ, D), 1.0),
        'x_sample': nrm(ks[1], (DEC_BATCH, DEC_SEQ, D), 1.0),
        'c_prompt': nrm(ks[2], (BATCH, D), 1.0),
        'c_sample': nrm(ks[3], (DEC_BATCH, D), 1.0),
        'ada_w': nrm(ks[4], (L, D, N_MOD * D), 0.5 * D ** -0.5),
        'ada_b': nrm(ks[5], (L, N_MOD * D), 0.02),
        'ffn1_norm': 1.0 + nrm(ks[6], (L, D), 0.02),
        'ffn1_wi': nrm(ks[7], (L, D, 2 * D_FF), D ** -0.5),
        'ffn1_wo': nrm(ks[8], (L, D_FF, D), D_FF ** -0.5),
        'mix_norm': 1.0 + nrm(ks[9], (L, D), 0.02),
        'w_in': nrm(ks[10], (L, D, IN_WIDTH), D ** -0.5),
        'ret_decay_logit': base[None] + nrm(ks[11], (L, 2, RET_HEADS), 0.1),
        'ret_gn_gain': 1.0 + nrm(ks[12], (L, RET_V_WIDTH), 0.02),
        'w_ret_o': nrm(ks[13], (L, RET_V_WIDTH, D), RET_V_WIDTH ** -0.5),
        'na_rpb': nrm(ks[14], (L, NA_HEADS, 2 * NA_KH_MAX - 1, 2 * NA_KW - 1), 0.1),
        'w_na_o': nrm(ks[15], (L, NA_WIDTH, D), NA_WIDTH ** -0.5),
        'w_out': nrm(ks[16], (L, D, D), D ** -0.5),
        'ffn2_norm': 1.0 + nrm(ks[17], (L, D), 0.02),
        'ffn2_wi': nrm(ks[18], (L, D, 2 * D_FF), D ** -0.5),
        'ffn2_wo': nrm(ks[19], (L, D_FF, D), D_FF ** -0.5),
        'final_norm': 1.0 + nrm(ks[20], (D,), 0.02),
    }


def reference(x_prompt, x_sample, c_prompt, c_sample, ada_w, ada_b, ffn1_norm, ffn1_wi, ffn1_wo,
              mix_norm, w_in, ret_decay_logit, ret_gn_gain, w_ret_o, na_rpb, w_na_o, w_out,
              ffn2_norm, ffn2_wi, ffn2_wo, final_norm):
    layer_weights = (ada_w, ada_b, ffn1_norm, ffn1_wi, ffn1_wo, mix_norm, w_in, ret_decay_logit,
                     ret_gn_gain, w_ret_o, na_rpb, w_na_o, w_out, ffn2_norm, ffn2_wi, ffn2_wo)
    y_prompt = trunk(x_prompt, c_prompt, layer_weights, final_norm)
    y_sample = trunk(x_sample, c_sample, layer_weights, final_norm)
    return (y_prompt, y_sample)
```

```python
import functools

import numpy as np
import jax
import jax.numpy as jnp
from jax import lax
from jax.experimental import pallas as pl
from jax.experimental.pallas import tpu as pltpu

F32 = jnp.float32
BF16 = jnp.bfloat16

D_MODEL = 1024
GRID_W = 64
RET_HEADS = 4
RET_QK_DIM = 128
RET_V_DIM = 256
RET_CHUNK = 128
RET_QK_WIDTH = RET_HEADS * RET_QK_DIM
RET_V_WIDTH = RET_HEADS * RET_V_DIM
NA_HEADS = 8
NA_HEAD_DIM = 64
NA_WIDTH = NA_HEADS * NA_HEAD_DIM
NA_KH = 8
NA_KW = 16
D_FF = 2816
N_MOD = 9
ROPE_BASE = 10000.0
RMS_EPS = 1e-6
GN_EPS = 1e-5
IN_WIDTH = 2 * RET_QK_WIDTH + 2 * RET_V_WIDTH + 3 * NA_WIDTH + 2 * D_MODEL
MASK_VALUE = -1e30

V7X_LANES = 128
V7X_VMEM_BYTES = 64 * 1024 * 1024
VMEM_LIMIT_BYTES = 56 * 1024 * 1024

TOKEN_TILE = 512
FFN_CHUNK = 256
PROJ_CHUNK = 512
NA_HEADS_PER_STEP = V7X_LANES // NA_HEAD_DIM


def _params(*semantics):
    return pltpu.CompilerParams(dimension_semantics=semantics, vmem_limit_bytes=VMEM_LIMIT_BYTES)


def _resident(shape):
    return pl.BlockSpec(shape, lambda *_: (0,) * len(shape), pipeline_mode=pl.Buffered(1))


def _silu(x):
    return x * jax.nn.sigmoid(x)


def _modulated_rms(x, gain, shift, scale):
    y = x * lax.rsqrt(jnp.mean(x * x, axis=-1, keepdims=True) + RMS_EPS) * gain
    return y * (1.0 + scale) + shift


def _mod_kernel(c_ref, w_ref, b_ref, o_ref):
    a = _silu(c_ref[...]).astype(BF16)
    o_ref[0] = jnp.dot(a, w_ref[0].astype(BF16), preferred_element_type=F32) + b_ref[0]


def _adaln(c, ada_w, ada_b):
    nb, d = c.shape
    depth, _, width = ada_w.shape
    tn = d
    return pl.pallas_call(
        _mod_kernel,
        grid=(depth, width // tn),
        in_specs=[
            pl.BlockSpec((nb, d), lambda l, j: (0, 0)),
            pl.BlockSpec((1, d, tn), lambda l, j: (l, 0, j)),
            pl.BlockSpec((1, 1, tn), lambda l, j: (l, 0, j)),
        ],
        out_specs=pl.BlockSpec((1, nb, tn), lambda l, j: (l, 0, j)),
        out_shape=jax.ShapeDtypeStruct((depth, nb, width), F32),
        compiler_params=_params("parallel", "parallel"),
        name="adaln_mod",
    )(c, ada_w, ada_b.reshape(depth, 1, width))


def _mod_spec(k):
    return pl.BlockSpec((1, 1, 1, D_MODEL), lambda b, t: (b, k, 0, 0))


def _ffn_kernel(h_ref, gain_ref, sh_ref, sc_ref, g_ref, wi_ref, wo_ref, *rest, final):
    if final:
        fgain_ref, o_ref, mid_ref = rest
    else:
        o_ref, mid_ref = rest
    x = h_ref[0]
    nb = _modulated_rms(x, gain_ref[...], sh_ref[0, 0], sc_ref[0, 0]).astype(BF16)
    for j in range(D_FF // FFN_CHUNK):
        lo = j * FFN_CHUNK
        a = jnp.dot(nb, wi_ref[:, lo:lo + FFN_CHUNK], preferred_element_type=F32)
        b = jnp.dot(nb, wi_ref[:, D_FF + lo:D_FF + lo + FFN_CHUNK], preferred_element_type=F32)
        mid_ref[:, lo:lo + FFN_CHUNK] = (_silu(a) * b).astype(BF16)
    y = jnp.dot(mid_ref[...], wo_ref[...], preferred_element_type=F32)
    out = x + (0.5 * g_ref[0, 0]) * y
    if final:
        out = out * lax.rsqrt(jnp.mean(out * out, axis=-1, keepdims=True) + RMS_EPS) * fgain_ref[...]
    o_ref[0] = out


def _ffn(h, mod, k0, gain, wi, wo, final_gain=None):
    bsz, t, d = h.shape
    tm = min(TOKEN_TILE, t)
    final = final_gain is not None
    tile = pl.BlockSpec((1, tm, d), lambda b, i: (b, i, 0))
    in_specs = [tile, _resident((1, d)), _mod_spec(k0), _mod_spec(k0 + 1), _mod_spec(k0 + 2),
                _resident(wi.shape), _resident(wo.shape)]
    args = [h, gain.reshape(1, d), mod, mod, mod, wi, wo]
    if final:
        in_specs.append(_resident((1, d)))
        args.append(final_gain.reshape(1, d))
    return pl.pallas_call(
        functools.partial(_ffn_kernel, final=final),
        grid=(bsz, t // tm),
        in_specs=in_specs,
        out_specs=tile,
        out_shape=jax.ShapeDtypeStruct(h.shape, F32),
        scratch_shapes=[pltpu.VMEM((tm, D_FF), BF16)],
        compiler_params=_params("parallel", "parallel"),
        name="ffn_final" if final else "ffn",
    )(*args)


def _rotary(y, cos, sin_signed):
    heads = []
    for hd in range(y.shape[1] // RET_QK_DIM):
        yh = y[:, hd * RET_QK_DIM:(hd + 1) * RET_QK_DIM]
        heads.append(yh * cos + pltpu.roll(yh, RET_QK_DIM // 2, axis=1) * sin_signed)
    return jnp.concatenate(heads, axis=1)


def _inproj_kernel(h_ref, gain_ref, sh_ref, sc_ref, cos_ref, sin_ref, w_ref,
                   qr_ref, kr_ref, vr_ref, gr_ref, qn_ref, kn_ref, vn_ref, gtr_ref, gtn_ref):
    nb = _modulated_rms(h_ref[0], gain_ref[...], sh_ref[0, 0], sc_ref[0, 0]).astype(BF16)
    cos = cos_ref[...]
    sin_signed = sin_ref[...]
    outs = (qr_ref, kr_ref, vr_ref, gr_ref, qn_ref, kn_ref, vn_ref, gtr_ref, gtn_ref)
    col = 0
    for idx, o_ref in enumerate(outs):
        width = o_ref.shape[-1]
        for lo in range(0, width, PROJ_CHUNK):
            y = jnp.dot(nb, w_ref[:, col + lo:col + lo + PROJ_CHUNK], preferred_element_type=F32)
            if idx == 0:
                y = _rotary(y, cos, sin_signed)
            elif idx == 1:
                y = _rotary(y, cos, sin_signed) * (RET_QK_DIM ** -0.5)
            o_ref[0, :, lo:lo + PROJ_CHUNK] = y.astype(BF16)
        col += width


def _rope_tables(t):
    half = RET_QK_DIM // 2
    inv = ROPE_BASE ** (-np.arange(half, dtype=np.float64) / half)
    ang = np.arange(t, dtype=np.float64)[:, None] * inv[None, :]
    cos = np.concatenate([np.cos(ang), np.cos(ang)], axis=1)
    sin_signed = np.concatenate([-np.sin(ang), np.sin(ang)], axis=1)
    return jnp.asarray(cos, F32), jnp.asarray(sin_signed, F32)


def _inproj(h, mod, gain, w_in):
    bsz, t, d = h.shape
    tm = min(TOKEN_TILE, t)
    cos, sin_signed = _rope_tables(t)
    widths = (RET_QK_WIDTH, RET_QK_WIDTH, RET_V_WIDTH, RET_V_WIDTH,
              NA_WIDTH, NA_WIDTH, NA_WIDTH, D_MODEL, D_MODEL)
    table = pl.BlockSpec((tm, RET_QK_DIM), lambda b, i: (i, 0))
    return pl.pallas_call(
        _inproj_kernel,
        grid=(bsz, t // tm),
        in_specs=[pl.BlockSpec((1, tm, d), lambda b, i: (b, i, 0)), _resident((1, d)),
                  _mod_spec(3), _mod_spec(4), table, table, _resident(w_in.shape)],
        out_specs=[pl.BlockSpec((1, tm, w), lambda b, i: (b, i, 0)) for w in widths],
        out_shape=[jax.ShapeDtypeStruct((bsz, t, w), BF16) for w in widths],
        compiler_params=_params("parallel", "parallel"),
        name="mixer_inproj",
    )(h, gain.reshape(1, d), mod, mod, cos, sin_signed, w_in)


def _log_sigmoid(x):
    return jnp.minimum(x, 0.0) - jnp.log1p(jnp.exp(-jnp.abs(x)))


def _retention_kernel(dl_ref, q_ref, k_ref, v_ref, g_ref, gain_ref, o_ref,
                      acc_ref, state_ref, decay_ref, xi_ref, zeta_ref):
    c = RET_CHUNK
    t = q_ref.shape[1]
    n_chunks = t // c
    hd = pl.program_id(1)

    def log_gamma(direction, shape):
        return _log_sigmoid(jnp.full(shape, dl_ref[direction, hd], F32))

    def position(shape):
        return lax.broadcasted_iota(jnp.int32, shape, 0).astype(F32)

    diff = (lax.broadcasted_iota(jnp.int32, (c, c), 0) - lax.broadcasted_iota(jnp.int32, (c, c), 1)).astype(F32)
    decay_ref[...] = jnp.exp(jnp.where(diff >= 0, diff * log_gamma(0, (c, c)), -diff * log_gamma(1, (c, c))))
    v_shape, k_shape = (c, RET_V_DIM), (c, RET_QK_DIM)
    xi_ref[0] = jnp.exp((position(v_shape) + 1.0) * log_gamma(0, v_shape))
    xi_ref[1] = jnp.exp((c - position(v_shape)) * log_gamma(1, v_shape))
    zeta_ref[0] = jnp.exp((c - 1.0 - position(k_shape)) * log_gamma(0, k_shape))
    zeta_ref[1] = jnp.exp(position(k_shape) * log_gamma(1, k_shape))
    gf_chunk = jnp.exp(c * log_gamma(0, (1, RET_V_DIM)))
    gb_chunk = jnp.exp(c * log_gamma(1, (1, RET_V_DIM)))

    def load(i):
        rows = pl.ds(pl.multiple_of(i * c, c), c)
        return rows, q_ref[0, rows, :], k_ref[0, rows, :], v_ref[0, rows, :]

    def update_state(k, v, zeta, g_chunk):
        kz_t = (k.astype(F32) * zeta).T.astype(BF16)
        state_ref[...] = state_ref[...] * g_chunk + jnp.dot(kz_t, v, preferred_element_type=F32)

    def fwd(i, carry):
        rows, q, k, v = load(i)
        s = lax.dot_general(q, k, (((1,), (1,)), ((), ())), preferred_element_type=F32)
        inner = jnp.dot((s * decay_ref[...]).astype(BF16), v, preferred_element_type=F32)
        cross = jnp.dot(q, state_ref[...].astype(BF16), preferred_element_type=F32) * xi_ref[0]
        acc_ref[rows, :] = inner + cross
        update_state(k, v, zeta_ref[0], gf_chunk)
        return carry

    def bwd(j, carry):
        i = n_chunks - 1 - j
        rows, q, k, v = load(i)
        cross = jnp.dot(q, state_ref[...].astype(BF16), preferred_element_type=F32) * xi_ref[1]
        y = acc_ref[rows, :] + cross
        mu = jnp.mean(y, axis=-1, keepdims=True)
        yc = y - mu
        var = jnp.mean(yc * yc, axis=-1, keepdims=True)
        yn = yc * lax.rsqrt(var + GN_EPS) * gain_ref[...]
        o_ref[0, rows, :] = (_silu(g_ref[0, rows, :].astype(F32)) * yn).astype(BF16)
        update_state(k, v, zeta_ref[1], gb_chunk)
        return carry

    state_ref[...] = jnp.zeros_like(state_ref)
    lax.fori_loop(0, n_chunks, fwd, 0)
    state_ref[...] = jnp.zeros_like(state_ref)
    lax.fori_loop(0, n_chunks, bwd, 0)


def _retention(q, k, v, g, decay_logit, gn_gain):
    bsz, t, _ = q.shape
    c = RET_CHUNK
    qk_spec = pl.BlockSpec((1, t, RET_QK_DIM), lambda b, h: (b, 0, h))
    v_spec = pl.BlockSpec((1, t, RET_V_DIM), lambda b, h: (b, 0, h))
    return pl.pallas_call(
        _retention_kernel,
        grid=(bsz, RET_HEADS),
        in_specs=[pl.BlockSpec(memory_space=pltpu.SMEM), qk_spec, qk_spec, v_spec, v_spec,
                  pl.BlockSpec((1, RET_V_DIM), lambda b, h: (0, h))],
        out_specs=v_spec,
        out_shape=jax.ShapeDtypeStruct((bsz, t, RET_V_WIDTH), BF16),
        scratch_shapes=[pltpu.VMEM((t, RET_V_DIM), F32),
                        pltpu.VMEM((RET_QK_DIM, RET_V_DIM), F32),
                        pltpu.VMEM((c, c), F32),
                        pltpu.VMEM((2, c, RET_V_DIM), F32),
                        pltpu.VMEM((2, c, RET_QK_DIM), F32)],
        compiler_params=_params("parallel", "parallel"),
        name="retention",
    )(decay_logit, q, k, v, g, gn_gain.reshape(1, RET_V_WIDTH))


def _na_bias_table(rpb):
    qc = np.arange(GRID_W)[:, None]
    kc = np.arange(GRID_W)[None, :]
    cstart = np.clip(qc - NA_KW // 2, 0, GRID_W - NA_KW)
    valid = (kc >= cstart) & (kc < cstart + NA_KW)
    cidx = np.clip(kc - qc + NA_KW - 1, 0, 2 * NA_KW - 2)
    row = np.arange(NA_KH)[:, None] + np.arange(NA_KH)[None, :]
    tab = rpb.astype(F32)[:, row][:, :, :, cidx]
    tab = jnp.where(valid, tab, MASK_VALUE)
    return tab.transpose(0, 1, 3, 2, 4).reshape(NA_HEADS, NA_KH, GRID_W, NA_KH * GRID_W)


def _na_kernel(q_ref, k_ref, v_ref, bias_ref, o_ref):
    w = GRID_W
    n_rows = q_ref.shape[1] // w
    first_head = lax.broadcasted_iota(jnp.int32, (w, V7X_LANES), 1) < NA_HEAD_DIM
    scale = NA_HEAD_DIM ** -0.5

    def row_step(r, carry):
        start = jnp.clip(r - NA_KH // 2, 0, n_rows - NA_KH)
        off = start - r + NA_KH - 1
        q = q_ref[0, pl.ds(pl.multiple_of(r * w, w), w), :]
        keys = pl.ds(pl.multiple_of(start * w, w), NA_KH * w)
        kw = k_ref[0, keys, :]
        vw = v_ref[0, keys, :]
        zero = jnp.zeros_like(q)
        q2 = jnp.concatenate([jnp.where(first_head, q, zero), jnp.where(first_head, zero, q)], axis=0)
        s = lax.dot_general(q2, kw, (((1,), (1,)), ((), ())), preferred_element_type=F32)
        s = s * scale + jnp.concatenate([bias_ref[0, off], bias_ref[1, off]], axis=0)
        p = jnp.exp(s - jnp.max(s, axis=-1, keepdims=True))
        denom = jnp.sum(p, axis=-1, keepdims=True)
        o = jnp.dot(p.astype(BF16), vw, preferred_element_type=F32) / denom
        o_ref[0, pl.ds(pl.multiple_of(r * w, w), w), :] = jnp.where(first_head, o[:w], o[w:]).astype(BF16)
        return carry

    lax.fori_loop(0, n_rows, row_step, 0)


def _neighborhood_attention(q, k, v, rpb):
    bsz, t, _ = q.shape
    assert t % GRID_W == 0 and t // GRID_W >= NA_KH
    bias = _na_bias_table(rpb)
    spec = pl.BlockSpec((1, t, V7X_LANES), lambda b, p: (b, 0, p))
    return pl.pallas_call(
        _na_kernel,
        grid=(bsz, NA_HEADS // NA_HEADS_PER_STEP),
        in_specs=[spec, spec, spec,
                  pl.BlockSpec((NA_HEADS_PER_STEP, NA_KH, GRID_W, NA_KH * GRID_W), lambda b, p: (p, 0, 0, 0))],
        out_specs=spec,
        out_shape=jax.ShapeDtypeStruct((bsz, t, NA_WIDTH), BF16),
        compiler_params=_params("parallel", "parallel"),
        name="neighborhood_attention",
    )(q, k, v, bias)


def _outproj_kernel(h_ref, g_ref, yg_ref, na_ref, gtr_ref, gtn_ref, wr_ref, wn_ref, wo_ref, o_ref):
    ret = jnp.dot(yg_ref[0], wr_ref[...], preferred_element_type=F32)
    na = jnp.dot(na_ref[0], wn_ref[...], preferred_element_type=F32)
    merged = (jax.nn.sigmoid(gtr_ref[0].astype(F32)) * ret
              + jax.nn.sigmoid(gtn_ref[0].astype(F32)) * na)
    out = jnp.dot(merged.astype(BF16), wo_ref[...], preferred_element_type=F32)
    o_ref[0] = h_ref[0] + g_ref[0, 0] * out


def _outproj(h, mod, yg, na, gate_r, gate_n, w_ret_o, w_na_o, w_out):
    bsz, t, d = h.shape
    tm = min(TOKEN_TILE, t)

    def tile(width):
        return pl.BlockSpec((1, tm, width), lambda b, i: (b, i, 0))

    return pl.pallas_call(
        _outproj_kernel,
        grid=(bsz, t // tm),
        in_specs=[tile(d), _mod_spec(5), tile(RET_V_WIDTH), tile(NA_WIDTH), tile(d), tile(d),
                  _resident(w_ret_o.shape), _resident(w_na_o.shape), _resident(w_out.shape)],
        out_specs=tile(d),
        out_shape=jax.ShapeDtypeStruct(h.shape, F32),
        compiler_params=_params("parallel", "parallel"),
        name="mixer_outproj",
    )(h, mod, yg, na, gate_r, gate_n, w_ret_o, w_na_o, w_out)


def _trunk(x, mods, weights, final_norm):
    assert x.shape[1] % RET_CHUNK == 0 and x.shape[2] == D_MODEL
    h = x
    depth = len(weights)
    for l, wl in enumerate(weights):
        (ffn1_norm, ffn1_wi, ffn1_wo, mix_norm, w_in, decay_logit, gn_gain, w_ret_o, rpb,
         w_na_o, w_out, ffn2_norm, ffn2_wi, ffn2_wo) = wl
        mod = mods[l]
        h = _ffn(h, mod, 0, ffn1_norm, ffn1_wi, ffn1_wo)
        q_r, k_r, v_r, g_r, q_n, k_n, v_n, gate_r, gate_n = _inproj(h, mod, mix_norm, w_in)
        yg = _retention(q_r, k_r, v_r, g_r, decay_logit, gn_gain)
        na = _neighborhood_attention(q_n, k_n, v_n, rpb)
        h = _outproj(h, mod, yg, na, gate_r, gate_n, w_ret_o, w_na_o, w_out)
        h = _ffn(h, mod, 6, ffn2_norm, ffn2_wi, ffn2_wo,
                 final_gain=final_norm if l == depth - 1 else None)
    return h


def kernel(x_prompt, x_sample, c_prompt, c_sample, ada_w, ada_b, ffn1_norm, ffn1_wi, ffn1_wo, mix_norm, w_in, ret_decay_logit, ret_gn_gain, w_ret_o, na_rpb, w_na_o, w_out, ffn2_norm, ffn2_wi, ffn2_wo, final_norm):
    depth = ada_w.shape[0]
    nbp = c_prompt.shape[0]
    mods = _adaln(jnp.concatenate([c_prompt, c_sample], axis=0), ada_w, ada_b)
    mods = mods.reshape(depth, mods.shape[1], N_MOD, 1, D_MODEL)
    bf = lambda w: w.astype(BF16)
    weights = [
        (ffn1_norm[l], bf(ffn1_wi[l]), bf(ffn1_wo[l]), mix_norm[l], bf(w_in[l]), ret_decay_logit[l],
         ret_gn_gain[l], bf(w_ret_o[l]), na_rpb[l], bf(w_na_o[l]), bf(w_out[l]),
         ffn2_norm[l], bf(ffn2_wi[l]), bf(ffn2_wo[l]))
        for l in range(depth)
    ]
    y_prompt = _trunk(x_prompt, mods[:, :nbp], weights, final_norm)
    y_sample = _trunk(x_sample, mods[:, nbp:], weights, final_norm)
    return (y_prompt, y_sample)
```

```python
import functools

import numpy as np
import jax
import jax.numpy as jnp
from jax import lax
from jax.experimental import pallas as pl
from jax.experimental.pallas import tpu as pltpu

F32 = jnp.float32
BF16 = jnp.bfloat16

D_MODEL = 1024
GRID_W = 64
RET_HEADS = 4
RET_QK_DIM = 128
RET_V_DIM = 256
RET_CHUNK = 128
RET_QK_WIDTH = RET_HEADS * RET_QK_DIM
RET_V_WIDTH = RET_HEADS * RET_V_DIM
NA_HEADS = 8
NA_HEAD_DIM = 64
NA_WIDTH = NA_HEADS * NA_HEAD_DIM
NA_KH = 8
NA_KW = 16
D_FF = 2816
N_MOD = 9
ROPE_BASE = 10000.0
RMS_EPS = 1e-6
GN_EPS = 1e-5
IN_WIDTH = 2 * RET_QK_WIDTH + 2 * RET_V_WIDTH + 3 * NA_WIDTH + 2 * D_MODEL
MASK_VALUE = -1e30
LOG2_E = 1.4426950408889634
NA_LOGIT_SCALE = NA_HEAD_DIM ** -0.5 * LOG2_E

V7X_LANES = 128
V7X_VMEM_BYTES = 64 * 1024 * 1024
VMEM_LIMIT_BYTES = 56 * 1024 * 1024

TOKEN_TILE = 512
FFN_CHUNK = 256
PROJ_CHUNK = 512
NA_HEADS_PER_STEP = V7X_LANES // NA_HEAD_DIM
NA_ROWS_PER_STAGE = 2
RET_CHUNKS_PER_STEP = 8


def _params(*semantics):
    return pltpu.CompilerParams(dimension_semantics=semantics, vmem_limit_bytes=VMEM_LIMIT_BYTES)


def _resident(shape):
    return pl.BlockSpec(shape, lambda *_: (0,) * len(shape), pipeline_mode=pl.Buffered(1))


def _silu(x):
    return x * jax.nn.sigmoid(x)


def _modulated_rms(x, gain, shift, scale):
    y = x * lax.rsqrt(jnp.mean(x * x, axis=-1, keepdims=True) + RMS_EPS) * gain
    return y * (1.0 + scale) + shift


def _largest_divisor(n, limit):
    return max(d for d in range(1, limit + 1) if n % d == 0)


def _mod_kernel(c_ref, w_ref, b_ref, o_ref):
    a = _silu(c_ref[...]).astype(BF16)
    o_ref[0] = jnp.dot(a, w_ref[0].astype(BF16), preferred_element_type=F32) + b_ref[0]


def _adaln(c, ada_w, ada_b):
    nb, d = c.shape
    depth, _, width = ada_w.shape
    tn = d
    return pl.pallas_call(
        _mod_kernel,
        grid=(depth, width // tn),
        in_specs=[
            pl.BlockSpec((nb, d), lambda l, j: (0, 0)),
            pl.BlockSpec((1, d, tn), lambda l, j: (l, 0, j)),
            pl.BlockSpec((1, 1, tn), lambda l, j: (l, 0, j)),
        ],
        out_specs=pl.BlockSpec((1, nb, tn), lambda l, j: (l, 0, j)),
        out_shape=jax.ShapeDtypeStruct((depth, nb, width), F32),
        compiler_params=_params("parallel", "parallel"),
        name="adaln_mod",
    )(c, ada_w, ada_b.reshape(depth, 1, width))


def _mod_spec(k):
    return pl.BlockSpec((1, 1, 1, D_MODEL), lambda b, t: (b, k, 0, 0))


def _ffn_kernel(h_ref, gain_ref, sh_ref, sc_ref, g_ref, wi_ref, wo_ref, *rest, final):
    if final:
        fgain_ref, o_ref, mid_ref = rest
    else:
        o_ref, mid_ref = rest
    x = h_ref[0]
    nb = _modulated_rms(x, gain_ref[...], sh_ref[0, 0], sc_ref[0, 0]).astype(BF16)
    for j in range(D_FF // FFN_CHUNK):
        lo = j * FFN_CHUNK
        a = jnp.dot(nb, wi_ref[:, lo:lo + FFN_CHUNK], preferred_element_type=F32)
        b = jnp.dot(nb, wi_ref[:, D_FF + lo:D_FF + lo + FFN_CHUNK], preferred_element_type=F32)
        mid_ref[:, lo:lo + FFN_CHUNK] = (_silu(a) * b).astype(BF16)
    y = jnp.dot(mid_ref[...], wo_ref[...], preferred_element_type=F32)
    out = x + (0.5 * g_ref[0, 0]) * y
    if final:
        out = out * lax.rsqrt(jnp.mean(out * out, axis=-1, keepdims=True) + RMS_EPS) * fgain_ref[...]
    o_ref[0] = out


def _ffn(h, mod, k0, gain, wi, wo, final_gain=None):
    bsz, t, d = h.shape
    tm = min(TOKEN_TILE, t)
    final = final_gain is not None
    tile = pl.BlockSpec((1, tm, d), lambda b, i: (b, i, 0))
    in_specs = [tile, _resident((1, d)), _mod_spec(k0), _mod_spec(k0 + 1), _mod_spec(k0 + 2),
                _resident(wi.shape), _resident(wo.shape)]
    args = [h, gain.reshape(1, d), mod, mod, mod, wi, wo]
    if final:
        in_specs.append(_resident((1, d)))
        args.append(final_gain.reshape(1, d))
    return pl.pallas_call(
        functools.partial(_ffn_kernel, final=final),
        grid=(bsz, t // tm),
        in_specs=in_specs,
        out_specs=tile,
        out_shape=jax.ShapeDtypeStruct(h.shape, F32),
        scratch_shapes=[pltpu.VMEM((tm, D_FF), BF16)],
        compiler_params=_params("parallel", "parallel"),
        name="ffn_final" if final else "ffn",
    )(*args)


def _rotary(y, cos, sin_signed):
    heads = []
    for hd in range(y.shape[1] // RET_QK_DIM):
        yh = y[:, hd * RET_QK_DIM:(hd + 1) * RET_QK_DIM]
        heads.append(yh * cos + pltpu.roll(yh, RET_QK_DIM // 2, axis=1) * sin_signed)
    return jnp.concatenate(heads, axis=1)


def _inproj_kernel(h_ref, gain_ref, sh_ref, sc_ref, cos_ref, sin_ref, w_ref,
                   qr_ref, kr_ref, vr_ref, gr_ref, qn_ref, kn_ref, vn_ref, gtr_ref, gtn_ref):
    nb = _modulated_rms(h_ref[0], gain_ref[...], sh_ref[0, 0], sc_ref[0, 0]).astype(BF16)
    cos = cos_ref[...]
    sin_signed = sin_ref[...]
    outs = (qr_ref, kr_ref, vr_ref, gr_ref, qn_ref, kn_ref, vn_ref, gtr_ref, gtn_ref)
    col = 0
    for idx, o_ref in enumerate(outs):
        width = o_ref.shape[-1]
        for lo in range(0, width, PROJ_CHUNK):
            y = jnp.dot(nb, w_ref[:, col + lo:col + lo + PROJ_CHUNK], preferred_element_type=F32)
            if idx == 0:
                y = _rotary(y, cos, sin_signed)
            elif idx == 1:
                y = _rotary(y, cos, sin_signed) * (RET_QK_DIM ** -0.5)
            elif idx == 4:
                y = y * NA_LOGIT_SCALE
            o_ref[0, :, lo:lo + PROJ_CHUNK] = y.astype(BF16)
        col += width


def _rope_tables(t):
    half = RET_QK_DIM // 2
    inv = ROPE_BASE ** (-np.arange(half, dtype=np.float64) / half)
    ang = np.arange(t, dtype=np.float64)[:, None] * inv[None, :]
    cos = np.concatenate([np.cos(ang), np.cos(ang)], axis=1)
    sin_signed = np.concatenate([-np.sin(ang), np.sin(ang)], axis=1)
    return jnp.asarray(cos, F32), jnp.asarray(sin_signed, F32)


def _inproj(h, mod, gain, w_in):
    bsz, t, d = h.shape
    tm = min(TOKEN_TILE, t)
    cos, sin_signed = _rope_tables(t)
    widths = (RET_QK_WIDTH, RET_QK_WIDTH, RET_V_WIDTH, RET_V_WIDTH,
              NA_WIDTH, NA_WIDTH, NA_WIDTH, D_MODEL, D_MODEL)
    table = pl.BlockSpec((tm, RET_QK_DIM), lambda b, i: (i, 0))
    return pl.pallas_call(
        _inproj_kernel,
        grid=(bsz, t // tm),
        in_specs=[pl.BlockSpec((1, tm, d), lambda b, i: (b, i, 0)), _resident((1, d)),
                  _mod_spec(3), _mod_spec(4), table, table, _resident(w_in.shape)],
        out_specs=[pl.BlockSpec((1, tm, w), lambda b, i: (b, i, 0)) for w in widths],
        out_shape=[jax.ShapeDtypeStruct((bsz, t, w), BF16) for w in widths],
        compiler_params=_params("parallel", "parallel"),
        name="mixer_inproj",
    )(h, gain.reshape(1, d), mod, mod, cos, sin_signed, w_in)


def _log_sigmoid(x):
    return jnp.minimum(x, 0.0) - jnp.log1p(jnp.exp(-jnp.abs(x)))


def _retention_kernel(dl_ref, q_ref, k_ref, v_ref, g_ref, gain_ref, o_ref,
                      inc_ref, pre_ref, state_ref, decay_ref, xi_ref, zeta_ref, *, group):
    c, dk = RET_CHUNK, RET_QK_DIM
    n_chunks = q_ref.shape[1] // c
    hd = pl.program_id(1)

    def log_gamma(direction, shape):
        return _log_sigmoid(jnp.full(shape, dl_ref[direction, hd], F32))

    def position(shape):
        return lax.broadcasted_iota(jnp.int32, shape, 0).astype(F32)

    diff = (lax.broadcasted_iota(jnp.int32, (c, c), 0) - lax.broadcasted_iota(jnp.int32, (c, c), 1)).astype(F32)
    decay_ref[...] = jnp.exp(jnp.where(diff >= 0, diff * log_gamma(0, (c, c)), -diff * log_gamma(1, (c, c))))
    k_shape = (c, dk)
    xi_ref[0] = jnp.exp((position(k_shape) + 1.0) * log_gamma(0, k_shape))
    xi_ref[1] = jnp.exp((c - position(k_shape)) * log_gamma(1, k_shape))
    zeta_ref[0] = jnp.exp((c - 1.0 - position(k_shape)) * log_gamma(0, k_shape))
    zeta_ref[1] = jnp.exp(position(k_shape) * log_gamma(1, k_shape))
    chunk_decay = [jnp.exp(c * log_gamma(d, (1, RET_V_DIM))) for d in range(2)]

    def chunk_rows(i):
        return pl.ds(pl.multiple_of(i * c, c), c)

    def increments(it, carry):
        for j in range(group):
            i = it * group + j
            k = k_ref[0, chunk_rows(i), :].astype(F32)
            kz = jnp.concatenate([k * zeta_ref[0], k * zeta_ref[1]], axis=1).astype(BF16)
            inc_ref[i] = lax.dot_general(kz, v_ref[0, chunk_rows(i), :], (((0,), (0,)), ((), ())),
                                         preferred_element_type=F32)
        return carry

    def scan(j, carry):
        for d, i in ((0, j), (1, n_chunks - 1 - j)):
            half = slice(d * dk, (d + 1) * dk)
            state = state_ref[d]
            pre_ref[i, half, :] = state.astype(BF16)
            state_ref[d] = state * chunk_decay[d] + inc_ref[i, half, :]
        return carry

    def outputs(it, carry):
        for j in range(group):
            i = it * group + j
            rows = chunk_rows(i)
            q = q_ref[0, rows, :]
            v = v_ref[0, rows, :]
            s = lax.dot_general(q, k_ref[0, rows, :], (((1,), (1,)), ((), ())), preferred_element_type=F32)
            qf = q.astype(F32)
            qx = jnp.concatenate([qf * xi_ref[0], qf * xi_ref[1]], axis=1).astype(BF16)
            y = (jnp.dot((s * decay_ref[...]).astype(BF16), v, preferred_element_type=F32)
                 + jnp.dot(qx, pre_ref[i], preferred_element_type=F32))
            yc = y - jnp.mean(y, axis=-1, keepdims=True)
            var = jnp.mean(yc * yc, axis=-1, keepdims=True)
            yn = yc * lax.rsqrt(var + GN_EPS) * gain_ref[...]
            o_ref[0, rows, :] = (_silu(g_ref[0, rows, :].astype(F32)) * yn).astype(BF16)
        return carry

    lax.fori_loop(0, n_chunks // group, increments, 0)
    state_ref[...] = jnp.zeros_like(state_ref)
    lax.fori_loop(0, n_chunks, scan, 0)
    lax.fori_loop(0, n_chunks // group, outputs, 0)


def _retention(q, k, v, g, decay_logit, gn_gain):
    bsz, t, _ = q.shape
    c = RET_CHUNK
    n_chunks = t // c
    qk_spec = pl.BlockSpec((1, t, RET_QK_DIM), lambda b, h: (b, 0, h))
    v_spec = pl.BlockSpec((1, t, RET_V_DIM), lambda b, h: (b, 0, h))
    return pl.pallas_call(
        functools.partial(_retention_kernel, group=_largest_divisor(n_chunks, RET_CHUNKS_PER_STEP)),
        grid=(bsz, RET_HEADS),
        in_specs=[pl.BlockSpec(memory_space=pltpu.SMEM), qk_spec, qk_spec, v_spec, v_spec,
                  pl.BlockSpec((1, RET_V_DIM), lambda b, h: (0, h))],
        out_specs=v_spec,
        out_shape=jax.ShapeDtypeStruct((bsz, t, RET_V_WIDTH), BF16),
        scratch_shapes=[pltpu.VMEM((n_chunks, 2 * RET_QK_DIM, RET_V_DIM), F32),
                        pltpu.VMEM((n_chunks, 2 * RET_QK_DIM, RET_V_DIM), BF16),
                        pltpu.VMEM((2, RET_QK_DIM, RET_V_DIM), F32),
                        pltpu.VMEM((c, c), F32),
                        pltpu.VMEM((2, c, RET_QK_DIM), F32),
                        pltpu.VMEM((2, c, RET_QK_DIM), F32)],
        compiler_params=_params("parallel", "parallel"),
        name="retention",
    )(decay_logit, q, k, v, g, gn_gain.reshape(1, RET_V_WIDTH))


def _na_bias_table(rpb):
    qc = np.arange(GRID_W)[:, None]
    kc = np.arange(GRID_W)[None, :]
    cstart = np.clip(qc - NA_KW // 2, 0, GRID_W - NA_KW)
    valid = (kc >= cstart) & (kc < cstart + NA_KW)
    cidx = np.clip(kc - qc + NA_KW - 1, 0, 2 * NA_KW - 2)
    row = np.arange(NA_KH)[:, None] + np.arange(NA_KH)[None, :]
    tab = (rpb.astype(F32) * LOG2_E)[:, row][:, :, :, cidx]
    tab = jnp.where(valid, tab, MASK_VALUE)
    return tab.transpose(0, 1, 3, 2, 4).reshape(NA_HEADS, NA_KH, GRID_W, NA_KH * GRID_W)


def _na_kernel(q_ref, k_ref, v_ref, bias_ref, o_ref, s_ref, p_ref, l_ref):
    w = GRID_W
    per = NA_ROWS_PER_STAGE
    n_rows = q_ref.shape[1] // w
    n_groups = n_rows // per
    first_head = lax.broadcasted_iota(jnp.int32, (w, V7X_LANES), 1) < NA_HEAD_DIM

    def window(r):
        start = jnp.clip(r - NA_KH // 2, 0, n_rows - NA_KH)
        return start - r + NA_KH - 1, pl.ds(pl.multiple_of(start * w, w), NA_KH * w)

    def query_rows(r):
        return pl.ds(pl.multiple_of(r * w, w), w)

    def scores(g):
        for i in range(per):
            r = g * per + i
            _, keys = window(r)
            q = q_ref[0, query_rows(r), :]
            zero = jnp.zeros_like(q)
            q2 = jnp.concatenate([jnp.where(first_head, q, zero), jnp.where(first_head, zero, q)], axis=0)
            s_ref[i] = lax.dot_general(q2, k_ref[0, keys, :], (((1,), (1,)), ((), ())),
                                       preferred_element_type=F32)

    def softmax(g):
        for i in range(per):
            off, _ = window(g * per + i)
            for hh in range(NA_HEADS_PER_STEP):
                rows = slice(hh * w, (hh + 1) * w)
                s = s_ref[i, rows, :] + bias_ref[hh, off]
                p = jnp.exp2(s - jnp.max(s, axis=-1, keepdims=True))
                l_ref[i, rows, :] = jnp.sum(p, axis=-1, keepdims=True)
                p_ref[i, rows, :] = p.astype(BF16)

    def output(g):
        for i in range(per):
            r = g * per + i
            _, keys = window(r)
            o = jnp.dot(p_ref[i], v_ref[0, keys, :], preferred_element_type=F32) / l_ref[i]
            o_ref[0, query_rows(r), :] = jnp.where(first_head, o[:w], o[w:]).astype(BF16)

    def steady(g, carry):
        output(g - 2)
        softmax(g - 1)
        scores(g)
        return carry

    scores(0)
    softmax(0)
    scores(1)
    lax.fori_loop(2, n_groups, steady, 0)
    output(n_groups - 2)
    softmax(n_groups - 1)
    output(n_groups - 1)


def _neighborhood_attention(q, k, v, rpb):
    bsz, t, _ = q.shape
    n_rows = t // GRID_W
    assert t % GRID_W == 0 and n_rows >= NA_KH and n_rows % NA_ROWS_PER_STAGE == 0
    assert n_rows // NA_ROWS_PER_STAGE >= 2
    bias = _na_bias_table(rpb)
    spec = pl.BlockSpec((1, t, V7X_LANES), lambda b, p: (b, 0, p))
    stage_rows = NA_HEADS_PER_STEP * GRID_W
    return pl.pallas_call(
        _na_kernel,
        grid=(bsz, NA_HEADS // NA_HEADS_PER_STEP),
        in_specs=[spec, spec, spec,
                  pl.BlockSpec((NA_HEADS_PER_STEP, NA_KH, GRID_W, NA_KH * GRID_W), lambda b, p: (p, 0, 0, 0))],
        out_specs=spec,
        out_shape=jax.ShapeDtypeStruct((bsz, t, NA_WIDTH), BF16),
        scratch_shapes=[pltpu.VMEM((NA_ROWS_PER_STAGE, stage_rows, NA_KH * GRID_W), F32),
                        pltpu.VMEM((NA_ROWS_PER_STAGE, stage_rows, NA_KH * GRID_W), BF16),
                        pltpu.VMEM((NA_ROWS_PER_STAGE, stage_rows, 1), F32)],
        compiler_params=_params("parallel", "parallel"),
        name="neighborhood_attention",
    )(q, k, v, bias)


def _outproj_kernel(h_ref, g_ref, yg_ref, na_ref, gtr_ref, gtn_ref, wr_ref, wn_ref, wo_ref, o_ref):
    ret = jnp.dot(yg_ref[0], wr_ref[...], preferred_element_type=F32)
    na = jnp.dot(na_ref[0], wn_ref[...], preferred_element_type=F32)
    merged = (jax.nn.sigmoid(gtr_ref[0].astype(F32)) * ret
              + jax.nn.sigmoid(gtn_ref[0].astype(F32)) * na)
    out = jnp.dot(merged.astype(BF16), wo_ref[...], preferred_element_type=F32)
    o_ref[0] = h_ref[0] + g_ref[0, 0] * out


def _outproj(h, mod, yg, na, gate_r, gate_n, w_ret_o, w_na_o, w_out):
    bsz, t, d = h.shape
    tm = min(TOKEN_TILE, t)

    def tile(width):
        return pl.BlockSpec((1, tm, width), lambda b, i: (b, i, 0))

    return pl.pallas_call(
        _outproj_kernel,
        grid=(bsz, t // tm),
        in_specs=[tile(d), _mod_spec(5), tile(RET_V_WIDTH), tile(NA_WIDTH), tile(d), tile(d),
                  _resident(w_ret_o.shape), _resident(w_na_o.shape), _resident(w_out.shape)],
        out_specs=tile(d),
        out_shape=jax.ShapeDtypeStruct(h.shape, F32),
        compiler_params=_params("parallel", "parallel"),
        name="mixer_outproj",
    )(h, mod, yg, na, gate_r, gate_n, w_ret_o, w_na_o, w_out)


def _trunk(x, mods, weights, final_norm):
    assert x.shape[1] % RET_CHUNK == 0 and x.shape[2] == D_MODEL
    h = x
    depth = len(weights)
    for l, wl in enumerate(weights):
        (ffn1_norm, ffn1_wi, ffn1_wo, mix_norm, w_in, decay_logit, gn_gain, w_ret_o, rpb,
         w_na_o, w_out, ffn2_norm, ffn2_wi, ffn2_wo) = wl
        mod = mods[l]
        h = _ffn(h, mod, 0, ffn1_norm, ffn1_wi, ffn1_wo)
        q_r, k_r, v_r, g_r, q_n, k_n, v_n, gate_r, gate_n = _inproj(h, mod, mix_norm, w_in)
        yg = _retention(q_r, k_r, v_r, g_r, decay_logit, gn_gain)
        na = _neighborhood_attention(q_n, k_n, v_n, rpb)
        h = _outproj(h, mod, yg, na, gate_r, gate_n, w_ret_o, w_na_o, w_out)
        h = _ffn(h, mod, 6, ffn2_norm, ffn2_wi, ffn2_wo,
                 final_gain=final_norm if l == depth - 1 else None)
    return h


def kernel(x_prompt, x_sample, c_prompt, c_sample, ada_w, ada_b, ffn1_norm, ffn1_wi, ffn1_wo, mix_norm, w_in, ret_decay_logit, ret_gn_gain, w_ret_o, na_rpb, w_na_o, w_out, ffn2_norm, ffn2_wi, ffn2_wo, final_norm):
    depth = ada_w.shape[0]
    nbp = c_prompt.shape[0]
    mods = _adaln(jnp.concatenate([c_prompt, c_sample], axis=0), ada_w, ada_b)
    mods = mods.reshape(depth, mods.shape[1], N_MOD, 1, D_MODEL)
    bf = lambda w: w.astype(BF16)
    weights = [
        (ffn1_norm[l], bf(ffn1_wi[l]), bf(ffn1_wo[l]), mix_norm[l], bf(w_in[l]), ret_decay_logit[l],
         ret_gn_gain[l], bf(w_ret_o[l]), na_rpb[l], bf(w_na_o[l]), bf(w_out[l]),
         ffn2_norm[l], bf(ffn2_wi[l]), bf(ffn2_wo[l]))
        for l in range(depth)
    ]
    y_prompt = _trunk(x_prompt, mods[:, :nbp], weights, final_norm)
    y_sample = _trunk(x_sample, mods[:, nbp:], weights, final_norm)
    return (y_prompt, y_sample)
```

```python
import functools

import numpy as np
import jax
import jax.numpy as jnp
from jax import lax
from jax.experimental import pallas as pl
from jax.experimental.pallas import tpu as pltpu

F32 = jnp.float32
BF16 = jnp.bfloat16

D_MODEL = 1024
GRID_W = 64
RET_HEADS = 4
RET_QK_DIM = 128
RET_V_DIM = 256
RET_CHUNK = 128
RET_QK_WIDTH = RET_HEADS * RET_QK_DIM
RET_V_WIDTH = RET_HEADS * RET_V_DIM
NA_HEADS = 8
NA_HEAD_DIM = 64
NA_WIDTH = NA_HEADS * NA_HEAD_DIM
NA_KH = 8
NA_KW = 16
D_FF = 2816
N_MOD = 9
ROPE_BASE = 10000.0
RMS_EPS = 1e-6
GN_EPS = 1e-5
IN_WIDTH = 2 * RET_QK_WIDTH + 2 * RET_V_WIDTH + 3 * NA_WIDTH + 2 * D_MODEL
MASK_VALUE = -1e30
LOG2_E = 1.4426950408889634
NA_LOGIT_SCALE = NA_HEAD_DIM ** -0.5 * LOG2_E

V7X_LANES = 128
V7X_VMEM_BYTES = 64 * 1024 * 1024
VMEM_LIMIT_BYTES = 56 * 1024 * 1024

TOKEN_TILE = 1024
FFN_CHUNK = 256
PROJ_CHUNK = 512
NA_HEADS_PER_STEP = V7X_LANES // NA_HEAD_DIM
NA_ROWS_PER_STAGE = 2
RET_BLOCK = 256
RET_BLOCKS_PER_STEP = 4


def _params(*semantics):
    return pltpu.CompilerParams(dimension_semantics=semantics, vmem_limit_bytes=VMEM_LIMIT_BYTES)


def _resident(shape):
    return pl.BlockSpec(shape, lambda *_: (0,) * len(shape), pipeline_mode=pl.Buffered(1))


def _silu(x):
    return x * jax.nn.sigmoid(x)


def _modulated_rms(x, gain, shift, scale):
    y = x * lax.rsqrt(jnp.mean(x * x, axis=-1, keepdims=True) + RMS_EPS) * gain
    return y * (1.0 + scale) + shift


def _largest_divisor(n, limit):
    return max(d for d in range(1, limit + 1) if n % d == 0)


def _mod_kernel(c_ref, w_ref, b_ref, o_ref):
    a = _silu(c_ref[...]).astype(BF16)
    o_ref[0] = jnp.dot(a, w_ref[0].astype(BF16), preferred_element_type=F32) + b_ref[0]


def _adaln(c, ada_w, ada_b):
    nb, d = c.shape
    depth, _, width = ada_w.shape
    tn = d
    return pl.pallas_call(
        _mod_kernel,
        grid=(depth, width // tn),
        in_specs=[
            pl.BlockSpec((nb, d), lambda l, j: (0, 0)),
            pl.BlockSpec((1, d, tn), lambda l, j: (l, 0, j)),
            pl.BlockSpec((1, 1, tn), lambda l, j: (l, 0, j)),
        ],
        out_specs=pl.BlockSpec((1, nb, tn), lambda l, j: (l, 0, j)),
        out_shape=jax.ShapeDtypeStruct((depth, nb, width), F32),
        compiler_params=_params("parallel", "parallel"),
        name="adaln_mod",
    )(c, ada_w, ada_b.reshape(depth, 1, width))


def _mod_spec(k):
    return pl.BlockSpec((1, 1, 1, D_MODEL), lambda b, t: (b, k, 0, 0))


def _ffn_kernel(h_ref, gain_ref, sh_ref, sc_ref, g_ref, wi_ref, wo_ref, *rest, final):
    if final:
        fgain_ref, o_ref, mid_ref = rest
    else:
        o_ref, mid_ref = rest
    x = h_ref[0]
    nb = _modulated_rms(x, gain_ref[...], sh_ref[0, 0], sc_ref[0, 0]).astype(BF16)
    for j in range(D_FF // FFN_CHUNK):
        lo = j * FFN_CHUNK
        a = jnp.dot(nb, wi_ref[:, lo:lo + FFN_CHUNK], preferred_element_type=F32)
        b = jnp.dot(nb, wi_ref[:, D_FF + lo:D_FF + lo + FFN_CHUNK], preferred_element_type=F32)
        mid_ref[:, lo:lo + FFN_CHUNK] = (_silu(a) * b).astype(BF16)
    y = jnp.dot(mid_ref[...], wo_ref[...], preferred_element_type=F32)
    out = x + (0.5 * g_ref[0, 0]) * y
    if final:
        out = out * lax.rsqrt(jnp.mean(out * out, axis=-1, keepdims=True) + RMS_EPS) * fgain_ref[...]
    o_ref[0] = out


def _ffn(h, mod, k0, gain, wi, wo, final_gain=None):
    bsz, t, d = h.shape
    tm = min(TOKEN_TILE, t)
    final = final_gain is not None
    tile = pl.BlockSpec((1, tm, d), lambda b, i: (b, i, 0))
    in_specs = [tile, _resident((1, d)), _mod_spec(k0), _mod_spec(k0 + 1), _mod_spec(k0 + 2),
                _resident(wi.shape), _resident(wo.shape)]
    args = [h, gain.reshape(1, d), mod, mod, mod, wi, wo]
    if final:
        in_specs.append(_resident((1, d)))
        args.append(final_gain.reshape(1, d))
    return pl.pallas_call(
        functools.partial(_ffn_kernel, final=final),
        grid=(bsz, t // tm),
        in_specs=in_specs,
        out_specs=tile,
        out_shape=jax.ShapeDtypeStruct(h.shape, F32),
        scratch_shapes=[pltpu.VMEM((tm, D_FF), BF16)],
        compiler_params=_params("parallel", "parallel"),
        name="ffn_final" if final else "ffn",
    )(*args)


def _rotary(y, cos, sin_signed):
    heads = []
    for hd in range(y.shape[1] // RET_QK_DIM):
        yh = y[:, hd * RET_QK_DIM:(hd + 1) * RET_QK_DIM]
        heads.append(yh * cos + pltpu.roll(yh, RET_QK_DIM // 2, axis=1) * sin_signed)
    return jnp.concatenate(heads, axis=1)


def _inproj_kernel(h_ref, gain_ref, sh_ref, sc_ref, cos_ref, sin_ref, w_ref,
                   qr_ref, kr_ref, vr_ref, gr_ref, qn_ref, kn_ref, vn_ref, gtr_ref, gtn_ref):
    nb = _modulated_rms(h_ref[0], gain_ref[...], sh_ref[0, 0], sc_ref[0, 0]).astype(BF16)
    cos = cos_ref[...]
    sin_signed = sin_ref[...]
    outs = (qr_ref, kr_ref, vr_ref, gr_ref, qn_ref, kn_ref, vn_ref, gtr_ref, gtn_ref)
    col = 0
    for idx, o_ref in enumerate(outs):
        width = o_ref.shape[-1]
        for lo in range(0, width, PROJ_CHUNK):
            y = jnp.dot(nb, w_ref[:, col + lo:col + lo + PROJ_CHUNK], preferred_element_type=F32)
            if idx == 0:
                y = _rotary(y, cos, sin_signed)
            elif idx == 1:
                y = _rotary(y, cos, sin_signed) * (RET_QK_DIM ** -0.5)
            elif idx == 3:
                y = _silu(y)
            elif idx == 4:
                y = y * NA_LOGIT_SCALE
            o_ref[0, :, lo:lo + PROJ_CHUNK] = y.astype(BF16)
        col += width


def _rope_tables(t):
    half = RET_QK_DIM // 2
    inv = ROPE_BASE ** (-np.arange(half, dtype=np.float64) / half)
    ang = np.arange(t, dtype=np.float64)[:, None] * inv[None, :]
    cos = np.concatenate([np.cos(ang), np.cos(ang)], axis=1)
    sin_signed = np.concatenate([-np.sin(ang), np.sin(ang)], axis=1)
    return jnp.asarray(cos, F32), jnp.asarray(sin_signed, F32)


def _inproj(h, mod, gain, w_in):
    bsz, t, d = h.shape
    tm = min(TOKEN_TILE, t)
    cos, sin_signed = _rope_tables(t)
    widths = (RET_QK_WIDTH, RET_QK_WIDTH, RET_V_WIDTH, RET_V_WIDTH,
              NA_WIDTH, NA_WIDTH, NA_WIDTH, D_MODEL, D_MODEL)
    table = pl.BlockSpec((tm, RET_QK_DIM), lambda b, i: (i, 0))
    return pl.pallas_call(
        _inproj_kernel,
        grid=(bsz, t // tm),
        in_specs=[pl.BlockSpec((1, tm, d), lambda b, i: (b, i, 0)), _resident((1, d)),
                  _mod_spec(3), _mod_spec(4), table, table, _resident(w_in.shape)],
        out_specs=[pl.BlockSpec((1, tm, w), lambda b, i: (b, i, 0)) for w in widths],
        out_shape=[jax.ShapeDtypeStruct((bsz, t, w), BF16) for w in widths],
        compiler_params=_params("parallel", "parallel"),
        name="mixer_inproj",
    )(h, gain.reshape(1, d), mod, mod, cos, sin_signed, w_in)


def _log_sigmoid(x):
    return jnp.minimum(x, 0.0) - jnp.log1p(jnp.exp(-jnp.abs(x)))


def _retention_kernel(dl_ref, q_ref, k_ref, v_ref, o_ref,
                      inc_ref, pre_ref, state_ref, decay_ref, xi_ref, zeta_ref, *, group):
    c, dk = RET_BLOCK, RET_QK_DIM
    n_chunks = q_ref.shape[1] // c
    hd = pl.program_id(1)

    def log_gamma(direction, shape):
        return _log_sigmoid(jnp.full(shape, dl_ref[direction, hd], F32))

    def position(shape):
        return lax.broadcasted_iota(jnp.int32, shape, 0).astype(F32)

    diff = (lax.broadcasted_iota(jnp.int32, (c, c), 0) - lax.broadcasted_iota(jnp.int32, (c, c), 1)).astype(F32)
    decay_ref[...] = jnp.exp(jnp.where(diff >= 0, diff * log_gamma(0, (c, c)), -diff * log_gamma(1, (c, c))))
    k_shape = (c, dk)
    xi_ref[0] = jnp.exp((position(k_shape) + 1.0) * log_gamma(0, k_shape))
    xi_ref[1] = jnp.exp((c - position(k_shape)) * log_gamma(1, k_shape))
    zeta_ref[0] = jnp.exp((c - 1.0 - position(k_shape)) * log_gamma(0, k_shape))
    zeta_ref[1] = jnp.exp(position(k_shape) * log_gamma(1, k_shape))
    chunk_decay = [jnp.exp(c * log_gamma(d, (1, RET_V_DIM))) for d in range(2)]

    def chunk_rows(i):
        return pl.ds(pl.multiple_of(i * c, c), c)

    def increments(it, carry):
        for j in range(group):
            i = it * group + j
            k = k_ref[0, chunk_rows(i), :].astype(F32)
            kz = jnp.concatenate([k * zeta_ref[0], k * zeta_ref[1]], axis=1).astype(BF16)
            inc_ref[i] = lax.dot_general(kz, v_ref[0, chunk_rows(i), :], (((0,), (0,)), ((), ())),
                                         preferred_element_type=F32)
        return carry

    def scan(j, carry):
        for d, i in ((0, j), (1, n_chunks - 1 - j)):
            half = slice(d * dk, (d + 1) * dk)
            state = state_ref[d]
            pre_ref[i, half, :] = state.astype(BF16)
            state_ref[d] = state * chunk_decay[d] + inc_ref[i, half, :]
        return carry

    def outputs(it, carry):
        for j in range(group):
            i = it * group + j
            rows = chunk_rows(i)
            q = q_ref[0, rows, :]
            v = v_ref[0, rows, :]
            s = lax.dot_general(q, k_ref[0, rows, :], (((1,), (1,)), ((), ())), preferred_element_type=F32)
            qf = q.astype(F32)
            qx = jnp.concatenate([qf * xi_ref[0], qf * xi_ref[1]], axis=1).astype(BF16)
            y = (jnp.dot((s * decay_ref[...]).astype(BF16), v, preferred_element_type=F32)
                 + jnp.dot(qx, pre_ref[i], preferred_element_type=F32))
            yc = y - jnp.mean(y, axis=-1, keepdims=True)
            var = jnp.mean(yc * yc, axis=-1, keepdims=True)
            o_ref[0, rows, :] = (yc * lax.rsqrt(var + GN_EPS)).astype(BF16)
        return carry

    lax.fori_loop(0, n_chunks // group, increments, 0)
    state_ref[...] = jnp.zeros_like(state_ref)
    lax.fori_loop(0, n_chunks, scan, 0)
    lax.fori_loop(0, n_chunks // group, outputs, 0)


def _retention(q, k, v, decay_logit):
    bsz, t, _ = q.shape
    c = RET_BLOCK
    assert t % c == 0
    n_chunks = t // c
    qk_spec = pl.BlockSpec((1, t, RET_QK_DIM), lambda b, h: (b, 0, h))
    v_spec = pl.BlockSpec((1, t, RET_V_DIM), lambda b, h: (b, 0, h))
    return pl.pallas_call(
        functools.partial(_retention_kernel, group=_largest_divisor(n_chunks, RET_BLOCKS_PER_STEP)),
        grid=(bsz, RET_HEADS),
        in_specs=[pl.BlockSpec(memory_space=pltpu.SMEM), qk_spec, qk_spec, v_spec],
        out_specs=v_spec,
        out_shape=jax.ShapeDtypeStruct((bsz, t, RET_V_WIDTH), BF16),
        scratch_shapes=[pltpu.VMEM((n_chunks, 2 * RET_QK_DIM, RET_V_DIM), F32),
                        pltpu.VMEM((n_chunks, 2 * RET_QK_DIM, RET_V_DIM), BF16),
                        pltpu.VMEM((2, RET_QK_DIM, RET_V_DIM), F32),
                        pltpu.VMEM((c, c), F32),
                        pltpu.VMEM((2, c, RET_QK_DIM), F32),
                        pltpu.VMEM((2, c, RET_QK_DIM), F32)],
        compiler_params=_params("parallel", "parallel"),
        name="retention",
    )(decay_logit, q, k, v)


def _na_bias_table(rpb):
    qc = np.arange(GRID_W)[:, None]
    kc = np.arange(GRID_W)[None, :]
    cstart = np.clip(qc - NA_KW // 2, 0, GRID_W - NA_KW)
    valid = (kc >= cstart) & (kc < cstart + NA_KW)
    cidx = np.clip(kc - qc + NA_KW - 1, 0, 2 * NA_KW - 2)
    onehot = (cidx[None] == np.arange(2 * NA_KW - 1)[:, None, None]).astype(np.float32)
    tab = jnp.einsum("hrd,dqk->hrqk", rpb.astype(F32) * LOG2_E, onehot, precision=lax.Precision.HIGHEST)
    tab = jnp.where(valid, tab, MASK_VALUE)
    return jnp.concatenate([tab[:, :-1], tab[:, 1:]], axis=-1)


def _na_kernel(q_ref, k_ref, v_ref, bias_ref, o_ref, s_ref, p_ref, l_ref):
    w = GRID_W
    per = NA_ROWS_PER_STAGE
    n_rows = q_ref.shape[1] // w
    n_groups = n_rows // per
    first_head = lax.broadcasted_iota(jnp.int32, (w, V7X_LANES), 1) < NA_HEAD_DIM

    def window(r):
        start = jnp.clip(r - NA_KH // 2, 0, n_rows - NA_KH)
        return start - r + NA_KH - 1, pl.ds(pl.multiple_of(start * w, w), NA_KH * w)

    def query_rows(r):
        return pl.ds(pl.multiple_of(r * w, w), w)

    def scores(g):
        for i in range(per):
            r = g * per + i
            _, keys = window(r)
            q = q_ref[0, query_rows(r), :]
            zero = jnp.zeros_like(q)
            q2 = jnp.concatenate([jnp.where(first_head, q, zero), jnp.where(first_head, zero, q)], axis=0)
            s_ref[i] = lax.dot_general(q2, k_ref[0, keys, :], (((1,), (1,)), ((), ())),
                                       preferred_element_type=F32)

    def softmax(g):
        for i in range(per):
            off, _ = window(g * per + i)
            for hh in range(NA_HEADS_PER_STEP):
                rows = slice(hh * w, (hh + 1) * w)
                bias = jnp.concatenate([bias_ref[hh, off + j] for j in range(0, NA_KH, 2)], axis=1)
                s = s_ref[i, rows, :] + bias
                p = jnp.exp2(s - jnp.max(s, axis=-1, keepdims=True))
                l_ref[i, rows, :] = jnp.sum(p, axis=-1, keepdims=True)
                p_ref[i, rows, :] = p.astype(BF16)

    def output(g):
        for i in range(per):
            r = g * per + i
            _, keys = window(r)
            o = jnp.dot(p_ref[i], v_ref[0, keys, :], preferred_element_type=F32) / l_ref[i]
            o_ref[0, query_rows(r), :] = jnp.where(first_head, o[:w], o[w:]).astype(BF16)

    def steady(g, carry):
        output(g - 2)
        softmax(g - 1)
        scores(g)
        return carry

    scores(0)
    softmax(0)
    scores(1)
    lax.fori_loop(2, n_groups, steady, 0)
    output(n_groups - 2)
    softmax(n_groups - 1)
    output(n_groups - 1)


def _neighborhood_attention(q, k, v, rpb):
    bsz, t, _ = q.shape
    n_rows = t // GRID_W
    assert t % GRID_W == 0 and n_rows >= NA_KH and n_rows % NA_ROWS_PER_STAGE == 0
    assert n_rows // NA_ROWS_PER_STAGE >= 2
    bias = _na_bias_table(rpb)
    spec = pl.BlockSpec((1, t, V7X_LANES), lambda b, p: (b, 0, p))
    stage_rows = NA_HEADS_PER_STEP * GRID_W
    return pl.pallas_call(
        _na_kernel,
        grid=(bsz, NA_HEADS // NA_HEADS_PER_STEP),
        in_specs=[spec, spec, spec,
                  pl.BlockSpec((NA_HEADS_PER_STEP,) + bias.shape[1:], lambda b, p: (p, 0, 0, 0))],
        out_specs=spec,
        out_shape=jax.ShapeDtypeStruct((bsz, t, NA_WIDTH), BF16),
        scratch_shapes=[pltpu.VMEM((NA_ROWS_PER_STAGE, stage_rows, NA_KH * GRID_W), F32),
                        pltpu.VMEM((NA_ROWS_PER_STAGE, stage_rows, NA_KH * GRID_W), BF16),
                        pltpu.VMEM((NA_ROWS_PER_STAGE, stage_rows, 1), F32)],
        compiler_params=_params("parallel", "parallel"),
        name="neighborhood_attention",
    )(q, k, v, bias)


def _outproj_kernel(h_ref, g_ref, yn_ref, sg_ref, gain_ref, na_ref, gtr_ref, gtn_ref,
                    wr_ref, wn_ref, wo_ref, o_ref):
    gated = sg_ref[0].astype(F32) * (yn_ref[0].astype(F32) * gain_ref[...])
    ret = jnp.dot(gated.astype(BF16), wr_ref[...], preferred_element_type=F32)
    na = jnp.dot(na_ref[0], wn_ref[...], preferred_element_type=F32)
    merged = (jax.nn.sigmoid(gtr_ref[0].astype(F32)) * ret
              + jax.nn.sigmoid(gtn_ref[0].astype(F32)) * na)
    out = jnp.dot(merged.astype(BF16), wo_ref[...], preferred_element_type=F32)
    o_ref[0] = h_ref[0] + g_ref[0, 0] * out


def _outproj(h, mod, yn, silu_g, gn_gain, na, gate_r, gate_n, w_ret_o, w_na_o, w_out):
    bsz, t, d = h.shape
    tm = min(TOKEN_TILE, t)

    def tile(width):
        return pl.BlockSpec((1, tm, width), lambda b, i: (b, i, 0))

    return pl.pallas_call(
        _outproj_kernel,
        grid=(bsz, t // tm),
        in_specs=[tile(d), _mod_spec(5), tile(RET_V_WIDTH), tile(RET_V_WIDTH), _resident((1, RET_V_WIDTH)),
                  tile(NA_WIDTH), tile(d), tile(d),
                  _resident(w_ret_o.shape), _resident(w_na_o.shape), _resident(w_out.shape)],
        out_specs=tile(d),
        out_shape=jax.ShapeDtypeStruct(h.shape, F32),
        compiler_params=_params("parallel", "parallel"),
        name="mixer_outproj",
    )(h, mod, yn, silu_g, gn_gain.reshape(1, RET_V_WIDTH), na, gate_r, gate_n, w_ret_o, w_na_o, w_out)


def _trunk(x, mods, weights, final_norm):
    assert x.shape[1] % RET_CHUNK == 0 and x.shape[2] == D_MODEL
    h = x
    depth = len(weights)
    for l, wl in enumerate(weights):
        (ffn1_norm, ffn1_wi, ffn1_wo, mix_norm, w_in, decay_logit, gn_gain, w_ret_o, rpb,
         w_na_o, w_out, ffn2_norm, ffn2_wi, ffn2_wo) = wl
        mod = mods[l]
        h = _ffn(h, mod, 0, ffn1_norm, ffn1_wi, ffn1_wo)
        q_r, k_r, v_r, silu_g, q_n, k_n, v_n, gate_r, gate_n = _inproj(h, mod, mix_norm, w_in)
        yn = _retention(q_r, k_r, v_r, decay_logit)
        na = _neighborhood_attention(q_n, k_n, v_n, rpb)
        h = _outproj(h, mod, yn, silu_g, gn_gain, na, gate_r, gate_n, w_ret_o, w_na_o, w_out)
        h = _ffn(h, mod, 6, ffn2_norm, ffn2_wi, ffn2_wo,
                 final_gain=final_norm if l == depth - 1 else None)
    return h


def kernel(x_prompt, x_sample, c_prompt, c_sample, ada_w, ada_b, ffn1_norm, ffn1_wi, ffn1_wo, mix_norm, w_in, ret_decay_logit, ret_gn_gain, w_ret_o, na_rpb, w_na_o, w_out, ffn2_norm, ffn2_wi, ffn2_wo, final_norm):
    depth = ada_w.shape[0]
    nbp = c_prompt.shape[0]
    mods = _adaln(jnp.concatenate([c_prompt, c_sample], axis=0), ada_w, ada_b)
    mods = mods.reshape(depth, mods.shape[1], N_MOD, 1, D_MODEL)
    bf = lambda w: w.astype(BF16)
    weights = [
        (ffn1_norm[l], bf(ffn1_wi[l]), bf(ffn1_wo[l]), mix_norm[l], bf(w_in[l]), ret_decay_logit[l],
         ret_gn_gain[l], bf(w_ret_o[l]), na_rpb[l], bf(w_na_o[l]), bf(w_out[l]),
         ffn2_norm[l], bf(ffn2_wi[l]), bf(ffn2_wo[l]))
        for l in range(depth)
    ]
    y_prompt = _trunk(x_prompt, mods[:, :nbp], weights, final_norm)
    y_sample = _trunk(x_sample, mods[:, nbp:], weights, final_norm)
    return (y_prompt, y_sample)
```

```python
import functools

import numpy as np
import jax
import jax.numpy as jnp
from jax import lax
from jax.experimental import pallas as pl
from jax.experimental.pallas import tpu as pltpu

F32 = jnp.float32
BF16 = jnp.bfloat16

D_MODEL = 1024
GRID_W = 64
RET_HEADS = 4
RET_QK_DIM = 128
RET_V_DIM = 256
RET_CHUNK = 128
RET_QK_WIDTH = RET_HEADS * RET_QK_DIM
RET_V_WIDTH = RET_HEADS * RET_V_DIM
NA_HEADS = 8
NA_HEAD_DIM = 64
NA_WIDTH = NA_HEADS * NA_HEAD_DIM
NA_KH = 8
NA_KW = 16
D_FF = 2816
N_MOD = 9
ROPE_BASE = 10000.0
RMS_EPS = 1e-6
GN_EPS = 1e-5
IN_WIDTH = 2 * RET_QK_WIDTH + 2 * RET_V_WIDTH + 3 * NA_WIDTH + 2 * D_MODEL
MASK_VALUE = -1e30
LOG2_E = 1.4426950408889634
NA_LOGIT_SCALE = NA_HEAD_DIM ** -0.5 * LOG2_E

V7X_LANES = 128
V7X_VMEM_BYTES = 64 * 1024 * 1024
VMEM_LIMIT_BYTES = 56 * 1024 * 1024

TOKEN_TILE = 1024
FFN_CHUNK = 256
PROJ_CHUNK = 512
NA_HEADS_PER_STEP = V7X_LANES // NA_HEAD_DIM
NA_ROWS_PER_STAGE = 2
RET_BLOCK = 256
RET_BLOCKS_PER_STEP = 16
FFN_ROW_SLABS = 4
INPROJ_ROW_SLABS = 4
OUTPROJ_ROW_SLABS = 2
NA_PIPELINE_UNROLL = 2


def _params(*semantics):
    return pltpu.CompilerParams(dimension_semantics=semantics, vmem_limit_bytes=VMEM_LIMIT_BYTES)


def _resident(shape):
    return pl.BlockSpec(shape, lambda *_: (0,) * len(shape), pipeline_mode=pl.Buffered(1))


def _silu(x):
    return x * jax.nn.sigmoid(x)


def _modulated_rms(x, gain, shift, scale):
    y = x * lax.rsqrt(jnp.mean(x * x, axis=-1, keepdims=True) + RMS_EPS) * gain
    return y * (1.0 + scale) + shift


def _largest_divisor(n, limit):
    return max(d for d in range(1, limit + 1) if n % d == 0)


def _mod_kernel(c_ref, w_ref, b_ref, o_ref):
    a = _silu(c_ref[...]).astype(BF16)
    o_ref[0] = jnp.dot(a, w_ref[0].astype(BF16), preferred_element_type=F32) + b_ref[0]


def _adaln(c, ada_w, ada_b):
    nb, d = c.shape
    depth, _, width = ada_w.shape
    tn = d
    return pl.pallas_call(
        _mod_kernel,
        grid=(depth, width // tn),
        in_specs=[
            pl.BlockSpec((nb, d), lambda l, j: (0, 0)),
            pl.BlockSpec((1, d, tn), lambda l, j: (l, 0, j)),
            pl.BlockSpec((1, 1, tn), lambda l, j: (l, 0, j)),
        ],
        out_specs=pl.BlockSpec((1, nb, tn), lambda l, j: (l, 0, j)),
        out_shape=jax.ShapeDtypeStruct((depth, nb, width), F32),
        compiler_params=_params("parallel", "parallel"),
        name="adaln_mod",
    )(c, ada_w, ada_b.reshape(depth, 1, width))


def _mod_spec(k):
    return pl.BlockSpec((1, 1, 1, D_MODEL), lambda b, t: (b, k, 0, 0))


def _ffn_kernel(h_ref, gain_ref, sh_ref, sc_ref, g_ref, wi_ref, wo_ref, *rest, final):
    if final:
        fgain_ref, o_ref, mid_ref = rest
    else:
        o_ref, mid_ref = rest
    slab = h_ref.shape[1] // FFN_ROW_SLABS
    for s in range(FFN_ROW_SLABS):
        rows = slice(s * slab, (s + 1) * slab)
        x = h_ref[0, rows, :]
        nb = _modulated_rms(x, gain_ref[...], sh_ref[0, 0], sc_ref[0, 0]).astype(BF16)
        for j in range(D_FF // FFN_CHUNK):
            lo = j * FFN_CHUNK
            a = jnp.dot(nb, wi_ref[:, lo:lo + FFN_CHUNK], preferred_element_type=F32)
            b = jnp.dot(nb, wi_ref[:, D_FF + lo:D_FF + lo + FFN_CHUNK], preferred_element_type=F32)
            mid_ref[rows, lo:lo + FFN_CHUNK] = (_silu(a) * b).astype(BF16)
        y = jnp.dot(mid_ref[rows, :], wo_ref[...], preferred_element_type=F32)
        out = x + (0.5 * g_ref[0, 0]) * y
        if final:
            out = out * lax.rsqrt(jnp.mean(out * out, axis=-1, keepdims=True) + RMS_EPS) * fgain_ref[...]
        o_ref[0, rows, :] = out


def _ffn(h, mod, k0, gain, wi, wo, final_gain=None):
    bsz, t, d = h.shape
    tm = min(TOKEN_TILE, t)
    final = final_gain is not None
    tile = pl.BlockSpec((1, tm, d), lambda b, i: (b, i, 0))
    in_specs = [tile, _resident((1, d)), _mod_spec(k0), _mod_spec(k0 + 1), _mod_spec(k0 + 2),
                _resident(wi.shape), _resident(wo.shape)]
    args = [h, gain.reshape(1, d), mod, mod, mod, wi, wo]
    if final:
        in_specs.append(_resident((1, d)))
        args.append(final_gain.reshape(1, d))
    return pl.pallas_call(
        functools.partial(_ffn_kernel, final=final),
        grid=(bsz, t // tm),
        in_specs=in_specs,
        out_specs=tile,
        out_shape=jax.ShapeDtypeStruct(h.shape, F32),
        scratch_shapes=[pltpu.VMEM((tm, D_FF), BF16)],
        compiler_params=_params("parallel", "parallel"),
        name="ffn_final" if final else "ffn",
    )(*args)


def _rotary(y, cos, sin_signed):
    heads = []
    for hd in range(y.shape[1] // RET_QK_DIM):
        yh = y[:, hd * RET_QK_DIM:(hd + 1) * RET_QK_DIM]
        heads.append(yh * cos + pltpu.roll(yh, RET_QK_DIM // 2, axis=1) * sin_signed)
    return jnp.concatenate(heads, axis=1)


def _inproj_kernel(h_ref, gain_ref, sh_ref, sc_ref, cos_ref, sin_ref, w_ref,
                   qr_ref, kr_ref, vr_ref, gr_ref, qn_ref, kn_ref, vn_ref, gtr_ref, gtn_ref):
    outs = (qr_ref, kr_ref, vr_ref, gr_ref, qn_ref, kn_ref, vn_ref, gtr_ref, gtn_ref)
    slab = h_ref.shape[1] // INPROJ_ROW_SLABS
    for s in range(INPROJ_ROW_SLABS):
        rows = slice(s * slab, (s + 1) * slab)
        nb = _modulated_rms(h_ref[0, rows, :], gain_ref[...], sh_ref[0, 0], sc_ref[0, 0]).astype(BF16)
        cos = cos_ref[rows, :]
        sin_signed = sin_ref[rows, :]
        col = 0
        for idx, o_ref in enumerate(outs):
            width = o_ref.shape[-1]
            for lo in range(0, width, PROJ_CHUNK):
                y = jnp.dot(nb, w_ref[:, col + lo:col + lo + PROJ_CHUNK], preferred_element_type=F32)
                if idx == 0:
                    y = _rotary(y, cos, sin_signed)
                elif idx == 1:
                    y = _rotary(y, cos, sin_signed) * (RET_QK_DIM ** -0.5)
                elif idx == 3:
                    y = _silu(y)
                elif idx == 4:
                    y = y * NA_LOGIT_SCALE
                o_ref[0, rows, lo:lo + PROJ_CHUNK] = y.astype(BF16)
            col += width


def _rope_tables(t):
    half = RET_QK_DIM // 2
    inv = ROPE_BASE ** (-np.arange(half, dtype=np.float64) / half)
    ang = np.arange(t, dtype=np.float64)[:, None] * inv[None, :]
    cos = np.concatenate([np.cos(ang), np.cos(ang)], axis=1)
    sin_signed = np.concatenate([-np.sin(ang), np.sin(ang)], axis=1)
    return jnp.asarray(cos, F32), jnp.asarray(sin_signed, F32)


def _inproj(h, mod, gain, w_in):
    bsz, t, d = h.shape
    tm = min(TOKEN_TILE, t)
    cos, sin_signed = _rope_tables(t)
    widths = (RET_QK_WIDTH, RET_QK_WIDTH, RET_V_WIDTH, RET_V_WIDTH,
              NA_WIDTH, NA_WIDTH, NA_WIDTH, D_MODEL, D_MODEL)
    table = pl.BlockSpec((tm, RET_QK_DIM), lambda b, i: (i, 0))
    return pl.pallas_call(
        _inproj_kernel,
        grid=(bsz, t // tm),
        in_specs=[pl.BlockSpec((1, tm, d), lambda b, i: (b, i, 0)), _resident((1, d)),
                  _mod_spec(3), _mod_spec(4), table, table, _resident(w_in.shape)],
        out_specs=[pl.BlockSpec((1, tm, w), lambda b, i: (b, i, 0)) for w in widths],
        out_shape=[jax.ShapeDtypeStruct((bsz, t, w), BF16) for w in widths],
        compiler_params=_params("parallel", "parallel"),
        name="mixer_inproj",
    )(h, gain.reshape(1, d), mod, mod, cos, sin_signed, w_in)


def _log_sigmoid(x):
    return jnp.minimum(x, 0.0) - jnp.log1p(jnp.exp(-jnp.abs(x)))


def _retention_kernel(dl_ref, q_ref, k_ref, v_ref, o_ref,
                      inc_ref, pre_ref, state_ref, decay_ref, xi_ref, zeta_ref, *, group):
    c, dk = RET_BLOCK, RET_QK_DIM
    n_chunks = q_ref.shape[1] // c
    hd = pl.program_id(1)

    def log_gamma(direction, shape):
        return _log_sigmoid(jnp.full(shape, dl_ref[direction, hd], F32))

    def position(shape):
        return lax.broadcasted_iota(jnp.int32, shape, 0).astype(F32)

    diff = (lax.broadcasted_iota(jnp.int32, (c, c), 0) - lax.broadcasted_iota(jnp.int32, (c, c), 1)).astype(F32)
    decay_ref[...] = jnp.exp(jnp.where(diff >= 0, diff * log_gamma(0, (c, c)), -diff * log_gamma(1, (c, c))))
    k_shape = (c, dk)
    xi_ref[0] = jnp.exp((position(k_shape) + 1.0) * log_gamma(0, k_shape))
    xi_ref[1] = jnp.exp((c - position(k_shape)) * log_gamma(1, k_shape))
    zeta_ref[0] = jnp.exp((c - 1.0 - position(k_shape)) * log_gamma(0, k_shape))
    zeta_ref[1] = jnp.exp(position(k_shape) * log_gamma(1, k_shape))
    chunk_decay = [jnp.exp(c * log_gamma(d, (1, RET_V_DIM))) for d in range(2)]

    def chunk_rows(i):
        return pl.ds(pl.multiple_of(i * c, c), c)

    def increments(it, carry):
        for j in range(group):
            i = it * group + j
            k = k_ref[0, chunk_rows(i), :].astype(F32)
            kz = jnp.concatenate([k * zeta_ref[0], k * zeta_ref[1]], axis=1).astype(BF16)
            inc_ref[i] = lax.dot_general(kz, v_ref[0, chunk_rows(i), :], (((0,), (0,)), ((), ())),
                                         preferred_element_type=F32)
        return carry

    def scan(j, carry):
        for d, i in ((0, j), (1, n_chunks - 1 - j)):
            half = slice(d * dk, (d + 1) * dk)
            state = state_ref[d]
            pre_ref[i, half, :] = state.astype(BF16)
            state_ref[d] = state * chunk_decay[d] + inc_ref[i, half, :]
        return carry

    def outputs(it, carry):
        for j in range(group):
            i = it * group + j
            rows = chunk_rows(i)
            q = q_ref[0, rows, :]
            v = v_ref[0, rows, :]
            s = lax.dot_general(q, k_ref[0, rows, :], (((1,), (1,)), ((), ())), preferred_element_type=F32)
            qf = q.astype(F32)
            qx = jnp.concatenate([qf * xi_ref[0], qf * xi_ref[1]], axis=1).astype(BF16)
            y = (jnp.dot((s * decay_ref[...]).astype(BF16), v, preferred_element_type=F32)
                 + jnp.dot(qx, pre_ref[i], preferred_element_type=F32))
            yc = y - jnp.mean(y, axis=-1, keepdims=True)
            var = jnp.mean(yc * yc, axis=-1, keepdims=True)
            o_ref[0, rows, :] = (yc * lax.rsqrt(var + GN_EPS)).astype(BF16)
        return carry

    lax.fori_loop(0, n_chunks // group, increments, 0)
    state_ref[...] = jnp.zeros_like(state_ref)
    lax.fori_loop(0, n_chunks, scan, 0)
    lax.fori_loop(0, n_chunks // group, outputs, 0)


def _retention(q, k, v, decay_logit):
    bsz, t, _ = q.shape
    c = RET_BLOCK
    assert t % c == 0
    n_chunks = t // c
    qk_spec = pl.BlockSpec((1, t, RET_QK_DIM), lambda b, h: (b, 0, h))
    v_spec = pl.BlockSpec((1, t, RET_V_DIM), lambda b, h: (b, 0, h))
    return pl.pallas_call(
        functools.partial(_retention_kernel, group=_largest_divisor(n_chunks, RET_BLOCKS_PER_STEP)),
        grid=(bsz, RET_HEADS),
        in_specs=[pl.BlockSpec(memory_space=pltpu.SMEM), qk_spec, qk_spec, v_spec],
        out_specs=v_spec,
        out_shape=jax.ShapeDtypeStruct((bsz, t, RET_V_WIDTH), BF16),
        scratch_shapes=[pltpu.VMEM((n_chunks, 2 * RET_QK_DIM, RET_V_DIM), F32),
                        pltpu.VMEM((n_chunks, 2 * RET_QK_DIM, RET_V_DIM), BF16),
                        pltpu.VMEM((2, RET_QK_DIM, RET_V_DIM), F32),
                        pltpu.VMEM((c, c), F32),
                        pltpu.VMEM((2, c, RET_QK_DIM), F32),
                        pltpu.VMEM((2, c, RET_QK_DIM), F32)],
        compiler_params=_params("parallel", "parallel"),
        name="retention",
    )(decay_logit, q, k, v)


def _na_bias_table(rpb):
    qc = np.arange(GRID_W)[:, None]
    kc = np.arange(GRID_W)[None, :]
    cstart = np.clip(qc - NA_KW // 2, 0, GRID_W - NA_KW)
    valid = (kc >= cstart) & (kc < cstart + NA_KW)
    cidx = np.clip(kc - qc + NA_KW - 1, 0, 2 * NA_KW - 2)
    onehot = (cidx[None] == np.arange(2 * NA_KW - 1)[:, None, None]).astype(np.float32)
    tab = jnp.einsum("hrd,dqk->hrqk", rpb.astype(F32) * LOG2_E, onehot, precision=lax.Precision.HIGHEST)
    tab = jnp.where(valid, tab, MASK_VALUE)
    return jnp.concatenate([tab[:, :-1], tab[:, 1:]], axis=-1)


def _na_kernel(q_ref, k_ref, v_ref, bias_ref, o_ref, s_ref, p_ref, l_ref):
    w = GRID_W
    per = NA_ROWS_PER_STAGE
    n_rows = q_ref.shape[1] // w
    n_groups = n_rows // per
    first_head = lax.broadcasted_iota(jnp.int32, (w, V7X_LANES), 1) < NA_HEAD_DIM

    def window(r):
        start = jnp.clip(r - NA_KH // 2, 0, n_rows - NA_KH)
        return start - r + NA_KH - 1, pl.ds(pl.multiple_of(start * w, w), NA_KH * w)

    def query_rows(r):
        return pl.ds(pl.multiple_of(r * w, w), w)

    def scores(g):
        for i in range(per):
            r = g * per + i
            _, keys = window(r)
            q = q_ref[0, query_rows(r), :]
            zero = jnp.zeros_like(q)
            q2 = jnp.concatenate([jnp.where(first_head, q, zero), jnp.where(first_head, zero, q)], axis=0)
            s_ref[i] = lax.dot_general(q2, k_ref[0, keys, :], (((1,), (1,)), ((), ())),
                                       preferred_element_type=F32)

    def softmax(g):
        for i in range(per):
            off, _ = window(g * per + i)
            for hh in range(NA_HEADS_PER_STEP):
                rows = slice(hh * w, (hh + 1) * w)
                bias = jnp.concatenate([bias_ref[hh, off + j] for j in range(0, NA_KH, 2)], axis=1)
                s = s_ref[i, rows, :] + bias
                p = jnp.exp2(s - jnp.max(s, axis=-1, keepdims=True))
                l_ref[i, rows, :] = jnp.sum(p, axis=-1, keepdims=True)
                p_ref[i, rows, :] = p.astype(BF16)

    def output(g):
        for i in range(per):
            r = g * per + i
            _, keys = window(r)
            o = jnp.dot(p_ref[i], v_ref[0, keys, :], preferred_element_type=F32) / l_ref[i]
            o_ref[0, query_rows(r), :] = jnp.where(first_head, o[:w], o[w:]).astype(BF16)

    def steady(g, carry):
        output(g - 2)
        softmax(g - 1)
        scores(g)
        return carry

    scores(0)
    softmax(0)
    scores(1)
    lax.fori_loop(2, n_groups, steady, 0, unroll=NA_PIPELINE_UNROLL)
    output(n_groups - 2)
    softmax(n_groups - 1)
    output(n_groups - 1)


def _neighborhood_attention(q, k, v, rpb):
    bsz, t, _ = q.shape
    n_rows = t // GRID_W
    assert t % GRID_W == 0 and n_rows >= NA_KH and n_rows % NA_ROWS_PER_STAGE == 0
    assert n_rows // NA_ROWS_PER_STAGE >= 2
    bias = _na_bias_table(rpb)
    spec = pl.BlockSpec((1, t, V7X_LANES), lambda b, p: (b, 0, p))
    stage_rows = NA_HEADS_PER_STEP * GRID_W
    return pl.pallas_call(
        _na_kernel,
        grid=(bsz, NA_HEADS // NA_HEADS_PER_STEP),
        in_specs=[spec, spec, spec,
                  pl.BlockSpec((NA_HEADS_PER_STEP,) + bias.shape[1:], lambda b, p: (p, 0, 0, 0))],
        out_specs=spec,
        out_shape=jax.ShapeDtypeStruct((bsz, t, NA_WIDTH), BF16),
        scratch_shapes=[pltpu.VMEM((NA_ROWS_PER_STAGE, stage_rows, NA_KH * GRID_W), F32),
                        pltpu.VMEM((NA_ROWS_PER_STAGE, stage_rows, NA_KH * GRID_W), BF16),
                        pltpu.VMEM((NA_ROWS_PER_STAGE, stage_rows, 1), F32)],
        compiler_params=_params("parallel", "parallel"),
        name="neighborhood_attention",
    )(q, k, v, bias)


def _outproj_kernel(h_ref, g_ref, yn_ref, sg_ref, gain_ref, na_ref, gtr_ref, gtn_ref,
                    wr_ref, wn_ref, wo_ref, o_ref):
    slab = h_ref.shape[1] // OUTPROJ_ROW_SLABS
    for s in range(OUTPROJ_ROW_SLABS):
        rows = slice(s * slab, (s + 1) * slab)
        gated = sg_ref[0, rows, :].astype(F32) * (yn_ref[0, rows, :].astype(F32) * gain_ref[...])
        ret = jnp.dot(gated.astype(BF16), wr_ref[...], preferred_element_type=F32)
        na = jnp.dot(na_ref[0, rows, :], wn_ref[...], preferred_element_type=F32)
        merged = (jax.nn.sigmoid(gtr_ref[0, rows, :].astype(F32)) * ret
                  + jax.nn.sigmoid(gtn_ref[0, rows, :].astype(F32)) * na)
        out = jnp.dot(merged.astype(BF16), wo_ref[...], preferred_element_type=F32)
        o_ref[0, rows, :] = h_ref[0, rows, :] + g_ref[0, 0] * out


def _outproj(h, mod, yn, silu_g, gn_gain, na, gate_r, gate_n, w_ret_o, w_na_o, w_out):
    bsz, t, d = h.shape
    tm = min(TOKEN_TILE, t)

    def tile(width):
        return pl.BlockSpec((1, tm, width), lambda b, i: (b, i, 0))

    return pl.pallas_call(
        _outproj_kernel,
        grid=(bsz, t // tm),
        in_specs=[tile(d), _mod_spec(5), tile(RET_V_WIDTH), tile(RET_V_WIDTH), _resident((1, RET_V_WIDTH)),
                  tile(NA_WIDTH), tile(d), tile(d),
                  _resident(w_ret_o.shape), _resident(w_na_o.shape), _resident(w_out.shape)],
        out_specs=tile(d),
        out_shape=jax.ShapeDtypeStruct(h.shape, F32),
        compiler_params=_params("parallel", "parallel"),
        name="mixer_outproj",
    )(h, mod, yn, silu_g, gn_gain.reshape(1, RET_V_WIDTH), na, gate_r, gate_n, w_ret_o, w_na_o, w_out)


def _trunk(x, mods, weights, final_norm):
    assert x.shape[1] % RET_CHUNK == 0 and x.shape[2] == D_MODEL
    h = x
    depth = len(weights)
    for l, wl in enumerate(weights):
        (ffn1_norm, ffn1_wi, ffn1_wo, mix_norm, w_in, decay_logit, gn_gain, w_ret_o, rpb,
         w_na_o, w_out, ffn2_norm, ffn2_wi, ffn2_wo) = wl
        mod = mods[l]
        h = _ffn(h, mod, 0, ffn1_norm, ffn1_wi, ffn1_wo)
        q_r, k_r, v_r, silu_g, q_n, k_n, v_n, gate_r, gate_n = _inproj(h, mod, mix_norm, w_in)
        yn = _retention(q_r, k_r, v_r, decay_logit)
        na = _neighborhood_attention(q_n, k_n, v_n, rpb)
        h = _outproj(h, mod, yn, silu_g, gn_gain, na, gate_r, gate_n, w_ret_o, w_na_o, w_out)
        h = _ffn(h, mod, 6, ffn2_norm, ffn2_wi, ffn2_wo,
                 final_gain=final_norm if l == depth - 1 else None)
    return h


def kernel(x_prompt, x_sample, c_prompt, c_sample, ada_w, ada_b, ffn1_norm, ffn1_wi, ffn1_wo, mix_norm, w_in, ret_decay_logit, ret_gn_gain, w_ret_o, na_rpb, w_na_o, w_out, ffn2_norm, ffn2_wi, ffn2_wo, final_norm):
    depth = ada_w.shape[0]
    nbp = c_prompt.shape[0]
    mods = _adaln(jnp.concatenate([c_prompt, c_sample], axis=0), ada_w, ada_b)
    mods = mods.reshape(depth, mods.shape[1], N_MOD, 1, D_MODEL)
    bf = lambda w: w.astype(BF16)
    weights = [
        (ffn1_norm[l], bf(ffn1_wi[l]), bf(ffn1_wo[l]), mix_norm[l], bf(w_in[l]), ret_decay_logit[l],
         ret_gn_gain[l], bf(w_ret_o[l]), na_rpb[l], bf(w_na_o[l]), bf(w_out[l]),
         ffn2_norm[l], bf(ffn2_wi[l]), bf(ffn2_wo[l]))
        for l in range(depth)
    ]
    y_prompt = _trunk(x_prompt, mods[:, :nbp], weights, final_norm)
    y_sample = _trunk(x_sample, mods[:, nbp:], weights, final_norm)
    return (y_prompt, y_sample)
```

```python
import functools

import numpy as np
import jax
import jax.numpy as jnp
from jax import lax
from jax.experimental import pallas as pl
from jax.experimental.pallas import tpu as pltpu

F32 = jnp.float32
BF16 = jnp.bfloat16

D_MODEL = 1024
GRID_W = 64
RET_HEADS = 4
RET_QK_DIM = 128
RET_V_DIM = 256
RET_CHUNK = 128
RET_QK_WIDTH = RET_HEADS * RET_QK_DIM
RET_V_WIDTH = RET_HEADS * RET_V_DIM
NA_HEADS = 8
NA_HEAD_DIM = 64
NA_WIDTH = NA_HEADS * NA_HEAD_DIM
NA_KH = 8
NA_KW = 16
D_FF = 2816
N_MOD = 9
ROPE_BASE = 10000.0
RMS_EPS = 1e-6
GN_EPS = 1e-5
IN_WIDTHS = (RET_QK_WIDTH, RET_QK_WIDTH, RET_V_WIDTH, RET_V_WIDTH,
             NA_WIDTH, NA_WIDTH, NA_WIDTH, D_MODEL, D_MODEL)
NA_OUTPUTS = (4, 5, 6)
MASK_VALUE = float("-inf")
LOG2_E = 1.4426950408889634
NA_LOGIT_SCALE = NA_HEAD_DIM ** -0.5 * LOG2_E

V7X_LANES = 128
V7X_VMEM_BYTES = 64 * 1024 * 1024
VMEM_LIMIT_BYTES = 56 * 1024 * 1024

TOKEN_TILE = 1024
FFN_CHUNK = 256
PROJ_CHUNK = 512
NA_HEADS_PER_STEP = V7X_LANES // NA_HEAD_DIM
NA_ROWS_PER_STAGE = 2
RET_BLOCK = 256
RET_BLOCKS_PER_STEP = 16
FFN_ROW_SLABS = 8
FFN_FINAL_ROW_SLABS = 4
INPROJ_ROW_SLABS = 4
OUTPROJ_ROW_SLABS = 2
NA_PIPELINE_UNROLL = 2


def _params(*semantics):
    return pltpu.CompilerParams(dimension_semantics=semantics, vmem_limit_bytes=VMEM_LIMIT_BYTES)


def _resident(shape):
    return pl.BlockSpec(shape, lambda *_: (0,) * len(shape), pipeline_mode=pl.Buffered(1))


def _silu(x):
    return x * jax.nn.sigmoid(x)


def _modulated_rms(x, gain, shift, scale):
    y = x * lax.rsqrt(jnp.mean(x * x, axis=-1, keepdims=True) + RMS_EPS) * gain
    return y * (1.0 + scale) + shift


def _largest_divisor(n, limit):
    return max(d for d in range(1, limit + 1) if n % d == 0)


def _mod_kernel(c_ref, w_ref, b_ref, o_ref):
    a = _silu(c_ref[...]).astype(BF16)
    o_ref[0] = jnp.dot(a, w_ref[0].astype(BF16), preferred_element_type=F32) + b_ref[0]


def _adaln(c, ada_w, ada_b):
    nb, d = c.shape
    depth, _, width = ada_w.shape
    tn = d
    return pl.pallas_call(
        _mod_kernel,
        grid=(depth, width // tn),
        in_specs=[
            pl.BlockSpec((nb, d), lambda l, j: (0, 0)),
            pl.BlockSpec((1, d, tn), lambda l, j: (l, 0, j)),
            pl.BlockSpec((1, 1, tn), lambda l, j: (l, 0, j)),
        ],
        out_specs=pl.BlockSpec((1, nb, tn), lambda l, j: (l, 0, j)),
        out_shape=jax.ShapeDtypeStruct((depth, nb, width), F32),
        compiler_params=_params("parallel", "parallel"),
        name="adaln_mod",
    )(c, ada_w, ada_b.reshape(depth, 1, width))


def _mod_spec(k):
    return pl.BlockSpec((1, 1, 1, D_MODEL), lambda b, t: (b, k, 0, 0))


def _ffn_kernel(h_ref, gain_ref, sh_ref, sc_ref, g_ref, wi_ref, wo_ref, *rest, final):
    if final:
        fgain_ref, o_ref, mid_ref = rest
    else:
        o_ref, mid_ref = rest
    n_slabs = FFN_FINAL_ROW_SLABS if final else FFN_ROW_SLABS
    slab = h_ref.shape[1] // n_slabs
    for s in range(n_slabs):
        rows = slice(s * slab, (s + 1) * slab)
        x = h_ref[0, rows, :]
        nb = _modulated_rms(x, gain_ref[...], sh_ref[0, 0], sc_ref[0, 0]).astype(BF16)
        for j in range(D_FF // FFN_CHUNK):
            lo = j * FFN_CHUNK
            a = jnp.dot(nb, wi_ref[:, lo:lo + FFN_CHUNK], preferred_element_type=F32)
            b = jnp.dot(nb, wi_ref[:, D_FF + lo:D_FF + lo + FFN_CHUNK], preferred_element_type=F32)
            mid_ref[rows, lo:lo + FFN_CHUNK] = (_silu(a) * b).astype(BF16)
        y = jnp.dot(mid_ref[rows, :], wo_ref[...], preferred_element_type=F32)
        out = x + (0.5 * g_ref[0, 0]) * y
        if final:
            out = out * lax.rsqrt(jnp.mean(out * out, axis=-1, keepdims=True) + RMS_EPS) * fgain_ref[...]
        o_ref[0, rows, :] = out


def _ffn(h, mod, k0, gain, wi, wo, final_gain=None):
    bsz, t, d = h.shape
    tm = min(TOKEN_TILE, t)
    final = final_gain is not None
    tile = pl.BlockSpec((1, tm, d), lambda b, i: (b, i, 0))
    in_specs = [tile, _resident((1, d)), _mod_spec(k0), _mod_spec(k0 + 1), _mod_spec(k0 + 2),
                _resident(wi.shape), _resident(wo.shape)]
    args = [h, gain.reshape(1, d), mod, mod, mod, wi, wo]
    if final:
        in_specs.append(_resident((1, d)))
        args.append(final_gain.reshape(1, d))
    return pl.pallas_call(
        functools.partial(_ffn_kernel, final=final),
        grid=(bsz, t // tm),
        in_specs=in_specs,
        out_specs=tile,
        out_shape=jax.ShapeDtypeStruct(h.shape, F32),
        scratch_shapes=[pltpu.VMEM((tm, D_FF), BF16)],
        compiler_params=_params("parallel", "parallel"),
        name="ffn_final" if final else "ffn",
    )(*args)


def _rotary(y, cos, sin_signed):
    heads = []
    for hd in range(y.shape[1] // RET_QK_DIM):
        yh = y[:, hd * RET_QK_DIM:(hd + 1) * RET_QK_DIM]
        heads.append(yh * cos + pltpu.roll(yh, RET_QK_DIM // 2, axis=1) * sin_signed)
    return jnp.concatenate(heads, axis=1)


def _inproj_kernel(h_ref, gain_ref, sh_ref, sc_ref, cos_ref, sin_ref, w_ref,
                   qr_ref, kr_ref, vr_ref, gr_ref, qn_ref, kn_ref, vn_ref, gtr_ref, gtn_ref):
    outs = (qr_ref, kr_ref, vr_ref, gr_ref, qn_ref, kn_ref, vn_ref, gtr_ref, gtn_ref)
    slab = h_ref.shape[1] // INPROJ_ROW_SLABS
    for s in range(INPROJ_ROW_SLABS):
        rows = slice(s * slab, (s + 1) * slab)
        nb = _modulated_rms(h_ref[0, rows, :], gain_ref[...], sh_ref[0, 0], sc_ref[0, 0]).astype(BF16)
        cos = cos_ref[rows, :]
        sin_signed = sin_ref[rows, :]
        col = 0
        for idx, (o_ref, width) in enumerate(zip(outs, IN_WIDTHS)):
            for lo in range(0, width, PROJ_CHUNK):
                y = jnp.dot(nb, w_ref[:, col + lo:col + lo + PROJ_CHUNK], preferred_element_type=F32)
                if idx == 0:
                    y = _rotary(y, cos, sin_signed)
                elif idx == 1:
                    y = _rotary(y, cos, sin_signed) * (RET_QK_DIM ** -0.5)
                elif idx == 3:
                    y = _silu(y)
                elif idx == 4:
                    y = y * NA_LOGIT_SCALE
                if idx in NA_OUTPUTS:
                    for j in range(PROJ_CHUNK // V7X_LANES):
                        o_ref[0, lo // V7X_LANES + j, rows, :] = (
                            y[:, j * V7X_LANES:(j + 1) * V7X_LANES].astype(BF16))
                else:
                    o_ref[0, rows, lo:lo + PROJ_CHUNK] = y.astype(BF16)
            col += width


def _rope_tables(t):
    half = RET_QK_DIM // 2
    inv = ROPE_BASE ** (-np.arange(half, dtype=np.float64) / half)
    ang = np.arange(t, dtype=np.float64)[:, None] * inv[None, :]
    cos = np.concatenate([np.cos(ang), np.cos(ang)], axis=1)
    sin_signed = np.concatenate([-np.sin(ang), np.sin(ang)], axis=1)
    return jnp.asarray(cos, F32), jnp.asarray(sin_signed, F32)


def _inproj(h, mod, gain, w_in):
    bsz, t, d = h.shape
    tm = min(TOKEN_TILE, t)
    cos, sin_signed = _rope_tables(t)
    table = pl.BlockSpec((tm, RET_QK_DIM), lambda b, i: (i, 0))
    out_specs, out_shape = [], []
    for idx, w in enumerate(IN_WIDTHS):
        if idx in NA_OUTPUTS:
            pairs = w // V7X_LANES
            out_specs.append(pl.BlockSpec((1, pairs, tm, V7X_LANES), lambda b, i: (b, 0, i, 0)))
            out_shape.append(jax.ShapeDtypeStruct((bsz, pairs, t, V7X_LANES), BF16))
        else:
            out_specs.append(pl.BlockSpec((1, tm, w), lambda b, i: (b, i, 0)))
            out_shape.append(jax.ShapeDtypeStruct((bsz, t, w), BF16))
    return pl.pallas_call(
        _inproj_kernel,
        grid=(bsz, t // tm),
        in_specs=[pl.BlockSpec((1, tm, d), lambda b, i: (b, i, 0)), _resident((1, d)),
                  _mod_spec(3), _mod_spec(4), table, table, _resident(w_in.shape)],
        out_specs=out_specs,
        out_shape=out_shape,
        compiler_params=_params("parallel", "parallel"),
        name="mixer_inproj",
    )(h, gain.reshape(1, d), mod, mod, cos, sin_signed, w_in)


def _log_sigmoid(x):
    return jnp.minimum(x, 0.0) - jnp.log1p(jnp.exp(-jnp.abs(x)))


def _retention_kernel(dl_ref, q_ref, k_ref, v_ref, o_ref,
                      inc_ref, pre_ref, state_ref, decay_ref, xi_ref, zeta_ref, *, group):
    c, dk = RET_BLOCK, RET_QK_DIM
    n_chunks = q_ref.shape[1] // c
    hd = pl.program_id(1)

    def log_gamma(direction, shape):
        return _log_sigmoid(jnp.full(shape, dl_ref[direction, hd], F32))

    def position(shape):
        return lax.broadcasted_iota(jnp.int32, shape, 0).astype(F32)

    diff = (lax.broadcasted_iota(jnp.int32, (c, c), 0) - lax.broadcasted_iota(jnp.int32, (c, c), 1)).astype(F32)
    decay_ref[...] = jnp.exp(jnp.where(diff >= 0, diff * log_gamma(0, (c, c)), -diff * log_gamma(1, (c, c))))
    k_shape = (c, dk)
    xi_ref[0] = jnp.exp((position(k_shape) + 1.0) * log_gamma(0, k_shape))
    xi_ref[1] = jnp.exp((c - position(k_shape)) * log_gamma(1, k_shape))
    zeta_ref[0] = jnp.exp((c - 1.0 - position(k_shape)) * log_gamma(0, k_shape))
    zeta_ref[1] = jnp.exp(position(k_shape) * log_gamma(1, k_shape))
    chunk_decay = [jnp.exp(c * log_gamma(d, (1, RET_V_DIM))) for d in range(2)]

    def chunk_rows(i):
        return pl.ds(pl.multiple_of(i * c, c), c)

    def increments(it, carry):
        for j in range(group):
            i = it * group + j
            k = k_ref[0, chunk_rows(i), :].astype(F32)
            kz = jnp.concatenate([k * zeta_ref[0], k * zeta_ref[1]], axis=1).astype(BF16)
            inc_ref[i] = lax.dot_general(kz, v_ref[0, chunk_rows(i), :], (((0,), (0,)), ((), ())),
                                         preferred_element_type=F32)
        return carry

    def scan(j, carry):
        for d, i in ((0, j), (1, n_chunks - 1 - j)):
            half = slice(d * dk, (d + 1) * dk)
            state = state_ref[d]
            pre_ref[i, half, :] = state.astype(BF16)
            state_ref[d] = state * chunk_decay[d] + inc_ref[i, half, :]
        return carry

    def outputs(it, carry):
        for j in range(group):
            i = it * group + j
            rows = chunk_rows(i)
            q = q_ref[0, rows, :]
            v = v_ref[0, rows, :]
            s = lax.dot_general(q, k_ref[0, rows, :], (((1,), (1,)), ((), ())), preferred_element_type=F32)
            qf = q.astype(F32)
            qx = jnp.concatenate([qf * xi_ref[0], qf * xi_ref[1]], axis=1).astype(BF16)
            y = (jnp.dot((s * decay_ref[...]).astype(BF16), v, preferred_element_type=F32)
                 + jnp.dot(qx, pre_ref[i], preferred_element_type=F32))
            yc = y - jnp.mean(y, axis=-1, keepdims=True)
            var = jnp.mean(yc * yc, axis=-1, keepdims=True)
            o_ref[0, rows, :] = (yc * lax.rsqrt(var + GN_EPS)).astype(BF16)
        return carry

    lax.fori_loop(0, n_chunks // group, increments, 0)
    state_ref[...] = jnp.zeros_like(state_ref)
    lax.fori_loop(0, n_chunks, scan, 0)
    lax.fori_loop(0, n_chunks // group, outputs, 0)


def _retention(q, k, v, decay_logit):
    bsz, t, _ = q.shape
    c = RET_BLOCK
    assert t % c == 0
    n_chunks = t // c
    qk_spec = pl.BlockSpec((1, t, RET_QK_DIM), lambda b, h: (b, 0, h))
    v_spec = pl.BlockSpec((1, t, RET_V_DIM), lambda b, h: (b, 0, h))
    return pl.pallas_call(
        functools.partial(_retention_kernel, group=_largest_divisor(n_chunks, RET_BLOCKS_PER_STEP)),
        grid=(bsz, RET_HEADS),
        in_specs=[pl.BlockSpec(memory_space=pltpu.SMEM), qk_spec, qk_spec, v_spec],
        out_specs=v_spec,
        out_shape=jax.ShapeDtypeStruct((bsz, t, RET_V_WIDTH), BF16),
        scratch_shapes=[pltpu.VMEM((n_chunks, 2 * RET_QK_DIM, RET_V_DIM), F32),
                        pltpu.VMEM((n_chunks, 2 * RET_QK_DIM, RET_V_DIM), BF16),
                        pltpu.VMEM((2, RET_QK_DIM, RET_V_DIM), F32),
                        pltpu.VMEM((c, c), F32),
                        pltpu.VMEM((2, c, RET_QK_DIM), F32),
                        pltpu.VMEM((2, c, RET_QK_DIM), F32)],
        compiler_params=_params("parallel", "parallel"),
        name="retention",
    )(decay_logit, q, k, v)


def _na_bias_table(rpb):
    qc = np.arange(GRID_W)[:, None]
    kc = np.arange(GRID_W)[None, :]
    cstart = np.clip(qc - NA_KW // 2, 0, GRID_W - NA_KW)
    valid = (kc >= cstart) & (kc < cstart + NA_KW)
    cidx = np.clip(kc - qc + NA_KW - 1, 0, 2 * NA_KW - 2)
    onehot = (cidx[None] == np.arange(2 * NA_KW - 1)[:, None, None]).astype(np.float32)
    tab = jnp.einsum("hrd,dqk->hrqk", rpb.astype(F32) * LOG2_E, onehot, precision=lax.Precision.HIGHEST)
    tab = jnp.where(valid, tab, MASK_VALUE)
    return jnp.concatenate([tab[:, :-1], tab[:, 1:]], axis=-1)


def _na_kernel(q_ref, k_ref, v_ref, bias_ref, o_ref, s_ref, p_ref, l_ref):
    w = GRID_W
    per = NA_ROWS_PER_STAGE
    n_pairs = q_ref.shape[1]
    n_rows = q_ref.shape[2] // w
    groups_per_pair = n_rows // per
    n_groups = n_pairs * groups_per_pair
    first_head = lax.broadcasted_iota(jnp.int32, (w, V7X_LANES), 1) < NA_HEAD_DIM

    def locate(g, i):
        g = jnp.int32(g)
        pair = lax.div(g, jnp.int32(groups_per_pair))
        r = (g - pair * groups_per_pair) * per + i
        start = jnp.clip(r - NA_KH // 2, 0, n_rows - NA_KH)
        return (pair, pl.ds(pl.multiple_of(r * w, w), w), start - r + NA_KH - 1,
                pl.ds(pl.multiple_of(start * w, w), NA_KH * w))

    def scores(g):
        for i in range(per):
            pair, rows, _, keys = locate(g, i)
            q = q_ref[0, pair, rows, :]
            zero = jnp.zeros_like(q)
            q2 = jnp.concatenate([jnp.where(first_head, q, zero), jnp.where(first_head, zero, q)], axis=0)
            s_ref[i] = lax.dot_general(q2, k_ref[0, pair, keys, :], (((1,), (1,)), ((), ())),
                                       preferred_element_type=F32)

    def softmax(g):
        for i in range(per):
            pair, _, off, _ = locate(g, i)
            for hh in range(NA_HEADS_PER_STEP):
                rows = slice(hh * w, (hh + 1) * w)
                head = pair * NA_HEADS_PER_STEP + hh
                bias = jnp.concatenate([bias_ref[head, off + j] for j in range(0, NA_KH, 2)], axis=1)
                s = s_ref[i, rows, :] + bias
                p = jnp.exp2(s - jnp.max(s, axis=-1, keepdims=True))
                l_ref[i, rows, :] = jnp.sum(p, axis=-1, keepdims=True)
                p_ref[i, rows, :] = p.astype(BF16)

    def output(g):
        for i in range(per):
            pair, rows, _, keys = locate(g, i)
            o = jnp.dot(p_ref[i], v_ref[0, pair, keys, :], preferred_element_type=F32) / l_ref[i]
            o_ref[0, pair, rows, :] = jnp.where(first_head, o[:w], o[w:]).astype(BF16)

    def steady(g, carry):
        output(g - 2)
        softmax(g - 1)
        scores(g)
        return carry

    scores(0)
    softmax(0)
    scores(1)
    lax.fori_loop(2, n_groups, steady, 0, unroll=NA_PIPELINE_UNROLL)
    output(n_groups - 2)
    softmax(n_groups - 1)
    output(n_groups - 1)


def _neighborhood_attention(q, k, v, rpb):
    bsz, n_pairs, t, _ = q.shape
    n_rows = t // GRID_W
    assert t % GRID_W == 0 and n_rows >= NA_KH and n_rows % NA_ROWS_PER_STAGE == 0
    assert n_rows // NA_ROWS_PER_STAGE >= 2 and n_pairs * NA_HEADS_PER_STEP == NA_HEADS
    bias = _na_bias_table(rpb)
    spec = pl.BlockSpec((1, n_pairs, t, V7X_LANES), lambda b: (b, 0, 0, 0))
    stage_rows = NA_HEADS_PER_STEP * GRID_W
    return pl.pallas_call(
        _na_kernel,
        grid=(bsz,),
        in_specs=[spec, spec, spec, _resident(bias.shape)],
        out_specs=spec,
        out_shape=jax.ShapeDtypeStruct(q.shape, BF16),
        scratch_shapes=[pltpu.VMEM((NA_ROWS_PER_STAGE, stage_rows, NA_KH * GRID_W), F32),
                        pltpu.VMEM((NA_ROWS_PER_STAGE, stage_rows, NA_KH * GRID_W), BF16),
                        pltpu.VMEM((NA_ROWS_PER_STAGE, stage_rows, 1), F32)],
        compiler_params=_params("parallel"),
        name="neighborhood_attention",
    )(q, k, v, bias)


def _outproj_kernel(h_ref, g_ref, yn_ref, sg_ref, gain_ref, na_ref, gtr_ref, gtn_ref,
                    wr_ref, wn_ref, wo_ref, o_ref):
    slab = h_ref.shape[1] // OUTPROJ_ROW_SLABS
    for s in range(OUTPROJ_ROW_SLABS):
        rows = slice(s * slab, (s + 1) * slab)
        gated = sg_ref[0, rows, :].astype(F32) * (yn_ref[0, rows, :].astype(F32) * gain_ref[...])
        ret = jnp.dot(gated.astype(BF16), wr_ref[...], preferred_element_type=F32)
        na_in = jnp.concatenate([na_ref[0, p, rows, :] for p in range(na_ref.shape[1])], axis=1)
        na = jnp.dot(na_in, wn_ref[...], preferred_element_type=F32)
        merged = (jax.nn.sigmoid(gtr_ref[0, rows, :].astype(F32)) * ret
                  + jax.nn.sigmoid(gtn_ref[0, rows, :].astype(F32)) * na)
        out = jnp.dot(merged.astype(BF16), wo_ref[...], preferred_element_type=F32)
        o_ref[0, rows, :] = h_ref[0, rows, :] + g_ref[0, 0] * out


def _outproj(h, mod, yn, silu_g, gn_gain, na, gate_r, gate_n, w_ret_o, w_na_o, w_out):
    bsz, t, d = h.shape
    tm = min(TOKEN_TILE, t)

    def tile(width):
        return pl.BlockSpec((1, tm, width), lambda b, i: (b, i, 0))

    return pl.pallas_call(
        _outproj_kernel,
        grid=(bsz, t // tm),
        in_specs=[tile(d), _mod_spec(5), tile(RET_V_WIDTH), tile(RET_V_WIDTH), _resident((1, RET_V_WIDTH)),
                  pl.BlockSpec((1, na.shape[1], tm, V7X_LANES), lambda b, i: (b, 0, i, 0)), tile(d), tile(d),
                  _resident(w_ret_o.shape), _resident(w_na_o.shape), _resident(w_out.shape)],
        out_specs=tile(d),
        out_shape=jax.ShapeDtypeStruct(h.shape, F32),
        compiler_params=_params("parallel", "parallel"),
        name="mixer_outproj",
    )(h, mod, yn, silu_g, gn_gain.reshape(1, RET_V_WIDTH), na, gate_r, gate_n, w_ret_o, w_na_o, w_out)


def _trunk(x, mods, weights, final_norm):
    assert x.shape[1] % RET_CHUNK == 0 and x.shape[2] == D_MODEL
    h = x
    depth = len(weights)
    for l, wl in enumerate(weights):
        (ffn1_norm, ffn1_wi, ffn1_wo, mix_norm, w_in, decay_logit, gn_gain, w_ret_o, rpb,
         w_na_o, w_out, ffn2_norm, ffn2_wi, ffn2_wo) = wl
        mod = mods[l]
        h = _ffn(h, mod, 0, ffn1_norm, ffn1_wi, ffn1_wo)
        q_r, k_r, v_r, silu_g, q_n, k_n, v_n, gate_r, gate_n = _inproj(h, mod, mix_norm, w_in)
        yn = _retention(q_r, k_r, v_r, decay_logit)
        na = _neighborhood_attention(q_n, k_n, v_n, rpb)
        h = _outproj(h, mod, yn, silu_g, gn_gain, na, gate_r, gate_n, w_ret_o, w_na_o, w_out)
        h = _ffn(h, mod, 6, ffn2_norm, ffn2_wi, ffn2_wo,
                 final_gain=final_norm if l == depth - 1 else None)
    return h


def kernel(x_prompt, x_sample, c_prompt, c_sample, ada_w, ada_b, ffn1_norm, ffn1_wi, ffn1_wo, mix_norm, w_in, ret_decay_logit, ret_gn_gain, w_ret_o, na_rpb, w_na_o, w_out, ffn2_norm, ffn2_wi, ffn2_wo, final_norm):
    depth = ada_w.shape[0]
    nbp = c_prompt.shape[0]
    mods = _adaln(jnp.concatenate([c_prompt, c_sample], axis=0), ada_w, ada_b)
    mods = mods.reshape(depth, mods.shape[1], N_MOD, 1, D_MODEL)
    bf = lambda w: w.astype(BF16)
    weights = [
        (ffn1_norm[l], bf(ffn1_wi[l]), bf(ffn1_wo[l]), mix_norm[l], bf(w_in[l]), ret_decay_logit[l],
         ret_gn_gain[l], bf(w_ret_o[l]), na_rpb[l], bf(w_na_o[l]), bf(w_out[l]),
         ffn2_norm[l], bf(ffn2_wi[l]), bf(ffn2_wo[l]))
        for l in range(depth)
    ]
    y_prompt = _trunk(x_prompt, mods[:, :nbp], weights, final_norm)
    y_sample = _trunk(x_sample, mods[:, nbp:], weights, final_norm)
    return (y_prompt, y_sample)
```

```python
import functools

import numpy as np
import jax
import jax.numpy as jnp
from jax import lax
from jax.experimental import pallas as pl
from jax.experimental.pallas import tpu as pltpu

F32 = jnp.float32
BF16 = jnp.bfloat16

D_MODEL = 1024
GRID_W = 64
RET_HEADS = 4
RET_QK_DIM = 128
RET_V_DIM = 256
RET_CHUNK = 128
RET_QK_WIDTH = RET_HEADS * RET_QK_DIM
RET_V_WIDTH = RET_HEADS * RET_V_DIM
NA_HEADS = 8
NA_HEAD_DIM = 64
NA_WIDTH = NA_HEADS * NA_HEAD_DIM
NA_KH = 8
NA_KW = 16
D_FF = 2816
N_MOD = 9
ROPE_BASE = 10000.0
RMS_EPS = 1e-6
GN_EPS = 1e-5
IN_WIDTHS = (RET_QK_WIDTH, RET_QK_WIDTH, RET_V_WIDTH, RET_V_WIDTH,
             NA_WIDTH, NA_WIDTH, NA_WIDTH, D_MODEL, D_MODEL)
NA_OUTPUTS = (4, 5, 6)
MASK_VALUE = float("-inf")
LOG2_E = 1.4426950408889634
NA_LOGIT_SCALE = NA_HEAD_DIM ** -0.5 * LOG2_E

V7X_LANES = 128
V7X_VMEM_BYTES = 64 * 1024 * 1024
VMEM_LIMIT_BYTES = 56 * 1024 * 1024

TOKEN_TILE = 1024
FFN_CHUNK = 256
PROJ_CHUNK = 512
NA_HEADS_PER_STEP = V7X_LANES // NA_HEAD_DIM
NA_ROWS_PER_STAGE = 4
RET_BLOCK = 256
RET_BLOCKS_PER_STEP = 16
FFN_ROW_SLABS = 8
FFN_FINAL_ROW_SLABS = 4
INPROJ_ROW_SLABS = 4
OUTPROJ_ROW_SLABS = 2


def _params(*semantics):
    return pltpu.CompilerParams(dimension_semantics=semantics, vmem_limit_bytes=VMEM_LIMIT_BYTES)


def _resident(shape):
    return pl.BlockSpec(shape, lambda *_: (0,) * len(shape), pipeline_mode=pl.Buffered(1))


def _silu(x):
    return x * jax.nn.sigmoid(x)


def _modulated_rms(x, gain, shift, scale):
    y = x * lax.rsqrt(jnp.mean(x * x, axis=-1, keepdims=True) + RMS_EPS) * gain
    return y * (1.0 + scale) + shift


def _largest_divisor(n, limit):
    return max(d for d in range(1, limit + 1) if n % d == 0)


def _mod_kernel(c_ref, w_ref, b_ref, o_ref):
    a = _silu(c_ref[...]).astype(BF16)
    o_ref[0] = jnp.dot(a, w_ref[0].astype(BF16), preferred_element_type=F32) + b_ref[0]


def _adaln(c, ada_w, ada_b):
    nb, d = c.shape
    depth, _, width = ada_w.shape
    tn = d
    return pl.pallas_call(
        _mod_kernel,
        grid=(depth, width // tn),
        in_specs=[
            pl.BlockSpec((nb, d), lambda l, j: (0, 0)),
            pl.BlockSpec((1, d, tn), lambda l, j: (l, 0, j)),
            pl.BlockSpec((1, 1, tn), lambda l, j: (l, 0, j)),
        ],
        out_specs=pl.BlockSpec((1, nb, tn), lambda l, j: (l, 0, j)),
        out_shape=jax.ShapeDtypeStruct((depth, nb, width), F32),
        compiler_params=_params("parallel", "parallel"),
        name="adaln_mod",
    )(c, ada_w, ada_b.reshape(depth, 1, width))


def _mod_spec(k):
    return pl.BlockSpec((1, 1, 1, D_MODEL), lambda b, t: (b, k, 0, 0))


def _ffn_kernel(h_ref, gain_ref, sh_ref, sc_ref, g_ref, wi_ref, wo_ref, *rest, final):
    if final:
        fgain_ref, o_ref, mid_ref = rest
    else:
        o_ref, mid_ref = rest
    n_slabs = FFN_FINAL_ROW_SLABS if final else FFN_ROW_SLABS
    slab = h_ref.shape[1] // n_slabs
    for s in range(n_slabs):
        rows = slice(s * slab, (s + 1) * slab)
        x = h_ref[0, rows, :]
        nb = _modulated_rms(x, gain_ref[...], sh_ref[0, 0], sc_ref[0, 0]).astype(BF16)
        for j in range(D_FF // FFN_CHUNK):
            lo = j * FFN_CHUNK
            a = jnp.dot(nb, wi_ref[:, lo:lo + FFN_CHUNK], preferred_element_type=F32)
            b = jnp.dot(nb, wi_ref[:, D_FF + lo:D_FF + lo + FFN_CHUNK], preferred_element_type=F32)
            mid_ref[rows, lo:lo + FFN_CHUNK] = (_silu(a) * b).astype(BF16)
        y = jnp.dot(mid_ref[rows, :], wo_ref[...], preferred_element_type=F32)
        out = x + (0.5 * g_ref[0, 0]) * y
        if final:
            out = out * lax.rsqrt(jnp.mean(out * out, axis=-1, keepdims=True) + RMS_EPS) * fgain_ref[...]
        o_ref[0, rows, :] = out


def _ffn(h, mod, k0, gain, wi, wo, final_gain=None):
    bsz, t, d = h.shape
    tm = min(TOKEN_TILE, t)
    final = final_gain is not None
    tile = pl.BlockSpec((1, tm, d), lambda b, i: (b, i, 0))
    in_specs = [tile, _resident((1, d)), _mod_spec(k0), _mod_spec(k0 + 1), _mod_spec(k0 + 2),
                _resident(wi.shape), _resident(wo.shape)]
    args = [h, gain.reshape(1, d), mod, mod, mod, wi, wo]
    if final:
        in_specs.append(_resident((1, d)))
        args.append(final_gain.reshape(1, d))
    return pl.pallas_call(
        functools.partial(_ffn_kernel, final=final),
        grid=(bsz, t // tm),
        in_specs=in_specs,
        out_specs=tile,
        out_shape=jax.ShapeDtypeStruct(h.shape, F32),
        scratch_shapes=[pltpu.VMEM((tm, D_FF), BF16)],
        compiler_params=_params("parallel", "parallel"),
        name="ffn_final" if final else "ffn",
    )(*args)


def _rotary(y, cos, sin_signed):
    heads = []
    for hd in range(y.shape[1] // RET_QK_DIM):
        yh = y[:, hd * RET_QK_DIM:(hd + 1) * RET_QK_DIM]
        heads.append(yh * cos + pltpu.roll(yh, RET_QK_DIM // 2, axis=1) * sin_signed)
    return jnp.concatenate(heads, axis=1)


def _inproj_kernel(h_ref, gain_ref, sh_ref, sc_ref, cos_ref, sin_ref, w_ref,
                   qr_ref, kr_ref, vr_ref, gr_ref, qn_ref, kn_ref, vn_ref, gtr_ref, gtn_ref):
    outs = (qr_ref, kr_ref, vr_ref, gr_ref, qn_ref, kn_ref, vn_ref, gtr_ref, gtn_ref)
    slab = h_ref.shape[1] // INPROJ_ROW_SLABS
    for s in range(INPROJ_ROW_SLABS):
        rows = slice(s * slab, (s + 1) * slab)
        nb = _modulated_rms(h_ref[0, rows, :], gain_ref[...], sh_ref[0, 0], sc_ref[0, 0]).astype(BF16)
        cos = cos_ref[rows, :]
        sin_signed = sin_ref[rows, :]
        col = 0
        for idx, (o_ref, width) in enumerate(zip(outs, IN_WIDTHS)):
            for lo in range(0, width, PROJ_CHUNK):
                y = jnp.dot(nb, w_ref[:, col + lo:col + lo + PROJ_CHUNK], preferred_element_type=F32)
                if idx == 0:
                    y = _rotary(y, cos, sin_signed)
                elif idx == 1:
                    y = _rotary(y, cos, sin_signed) * (RET_QK_DIM ** -0.5)
                elif idx == 3:
                    y = _silu(y)
                elif idx == 4:
                    y = y * NA_LOGIT_SCALE
                if idx in NA_OUTPUTS:
                    for j in range(PROJ_CHUNK // V7X_LANES):
                        o_ref[0, lo // V7X_LANES + j, rows, :] = (
                            y[:, j * V7X_LANES:(j + 1) * V7X_LANES].astype(BF16))
                else:
                    o_ref[0, rows, lo:lo + PROJ_CHUNK] = y.astype(BF16)
            col += width


def _rope_tables(t):
    half = RET_QK_DIM // 2
    inv = ROPE_BASE ** (-np.arange(half, dtype=np.float64) / half)
    ang = np.arange(t, dtype=np.float64)[:, None] * inv[None, :]
    cos = np.concatenate([np.cos(ang), np.cos(ang)], axis=1)
    sin_signed = np.concatenate([-np.sin(ang), np.sin(ang)], axis=1)
    return jnp.asarray(cos, F32), jnp.asarray(sin_signed, F32)


def _inproj(h, mod, gain, w_in):
    bsz, t, d = h.shape
    tm = min(TOKEN_TILE, t)
    cos, sin_signed = _rope_tables(t)
    table = pl.BlockSpec((tm, RET_QK_DIM), lambda b, i: (i, 0))
    out_specs, out_shape = [], []
    for idx, w in enumerate(IN_WIDTHS):
        if idx in NA_OUTPUTS:
            pairs = w // V7X_LANES
            out_specs.append(pl.BlockSpec((1, pairs, tm, V7X_LANES), lambda b, i: (b, 0, i, 0)))
            out_shape.append(jax.ShapeDtypeStruct((bsz, pairs, t, V7X_LANES), BF16))
        else:
            out_specs.append(pl.BlockSpec((1, tm, w), lambda b, i: (b, i, 0)))
            out_shape.append(jax.ShapeDtypeStruct((bsz, t, w), BF16))
    return pl.pallas_call(
        _inproj_kernel,
        grid=(bsz, t // tm),
        in_specs=[pl.BlockSpec((1, tm, d), lambda b, i: (b, i, 0)), _resident((1, d)),
                  _mod_spec(3), _mod_spec(4), table, table, _resident(w_in.shape)],
        out_specs=out_specs,
        out_shape=out_shape,
        compiler_params=_params("parallel", "parallel"),
        name="mixer_inproj",
    )(h, gain.reshape(1, d), mod, mod, cos, sin_signed, w_in)


def _log_sigmoid(x):
    return jnp.minimum(x, 0.0) - jnp.log1p(jnp.exp(-jnp.abs(x)))


def _retention_kernel(dl_ref, q_ref, k_ref, v_ref, o_ref,
                      inc_ref, pre_ref, state_ref, decay_ref, xi_ref, zeta_ref, *, group):
    c, dk = RET_BLOCK, RET_QK_DIM
    n_chunks = q_ref.shape[1] // c
    hd = pl.program_id(1)

    def log_gamma(direction, shape):
        return _log_sigmoid(jnp.full(shape, dl_ref[direction, hd], F32))

    def position(shape):
        return lax.broadcasted_iota(jnp.int32, shape, 0).astype(F32)

    diff = (lax.broadcasted_iota(jnp.int32, (c, c), 0) - lax.broadcasted_iota(jnp.int32, (c, c), 1)).astype(F32)
    decay_ref[...] = jnp.exp(jnp.where(diff >= 0, diff * log_gamma(0, (c, c)), -diff * log_gamma(1, (c, c))))
    k_shape = (c, dk)
    xi_ref[0] = jnp.exp((position(k_shape) + 1.0) * log_gamma(0, k_shape))
    xi_ref[1] = jnp.exp((c - position(k_shape)) * log_gamma(1, k_shape))
    zeta_ref[0] = jnp.exp((c - 1.0 - position(k_shape)) * log_gamma(0, k_shape))
    zeta_ref[1] = jnp.exp(position(k_shape) * log_gamma(1, k_shape))
    chunk_decay = [jnp.exp(c * log_gamma(d, (1, RET_V_DIM))) for d in range(2)]

    def chunk_rows(i):
        return pl.ds(pl.multiple_of(i * c, c), c)

    def increments(it, carry):
        for j in range(group):
            i = it * group + j
            k = k_ref[0, chunk_rows(i), :].astype(F32)
            kz = jnp.concatenate([k * zeta_ref[0], k * zeta_ref[1]], axis=1).astype(BF16)
            inc_ref[i] = lax.dot_general(kz, v_ref[0, chunk_rows(i), :], (((0,), (0,)), ((), ())),
                                         preferred_element_type=F32)
        return carry

    def scan(j, carry):
        for d, i in ((0, j), (1, n_chunks - 1 - j)):
            half = slice(d * dk, (d + 1) * dk)
            state = state_ref[d]
            pre_ref[i, half, :] = state.astype(BF16)
            state_ref[d] = state * chunk_decay[d] + inc_ref[i, half, :]
        return carry

    def outputs(it, carry):
        for j in range(group):
            i = it * group + j
            rows = chunk_rows(i)
            q = q_ref[0, rows, :]
            v = v_ref[0, rows, :]
            s = lax.dot_general(q, k_ref[0, rows, :], (((1,), (1,)), ((), ())), preferred_element_type=F32)
            qf = q.astype(F32)
            qx = jnp.concatenate([qf * xi_ref[0], qf * xi_ref[1]], axis=1).astype(BF16)
            y = (jnp.dot((s * decay_ref[...]).astype(BF16), v, preferred_element_type=F32)
                 + jnp.dot(qx, pre_ref[i], preferred_element_type=F32))
            yc = y - jnp.mean(y, axis=-1, keepdims=True)
            var = jnp.mean(yc * yc, axis=-1, keepdims=True)
            o_ref[0, rows, :] = (yc * lax.rsqrt(var + GN_EPS)).astype(BF16)
        return carry

    lax.fori_loop(0, n_chunks // group, increments, 0)
    state_ref[...] = jnp.zeros_like(state_ref)
    lax.fori_loop(0, n_chunks, scan, 0)
    lax.fori_loop(0, n_chunks // group, outputs, 0)


def _retention(q, k, v, decay_logit):
    bsz, t, _ = q.shape
    c = RET_BLOCK
    assert t % c == 0
    n_chunks = t // c
    qk_spec = pl.BlockSpec((1, t, RET_QK_DIM), lambda b, h: (b, 0, h))
    v_spec = pl.BlockSpec((1, t, RET_V_DIM), lambda b, h: (b, 0, h))
    return pl.pallas_call(
        functools.partial(_retention_kernel, group=_largest_divisor(n_chunks, RET_BLOCKS_PER_STEP)),
        grid=(bsz, RET_HEADS),
        in_specs=[pl.BlockSpec(memory_space=pltpu.SMEM), qk_spec, qk_spec, v_spec],
        out_specs=v_spec,
        out_shape=jax.ShapeDtypeStruct((bsz, t, RET_V_WIDTH), BF16),
        scratch_shapes=[pltpu.VMEM((n_chunks, 2 * RET_QK_DIM, RET_V_DIM), F32),
                        pltpu.VMEM((n_chunks, 2 * RET_QK_DIM, RET_V_DIM), BF16),
                        pltpu.VMEM((2, RET_QK_DIM, RET_V_DIM), F32),
                        pltpu.VMEM((c, c), F32),
                        pltpu.VMEM((2, c, RET_QK_DIM), F32),
                        pltpu.VMEM((2, c, RET_QK_DIM), F32)],
        compiler_params=_params("parallel", "parallel"),
        name="retention",
    )(decay_logit, q, k, v)


def _na_bias_table(rpb):
    qc = np.arange(GRID_W)[:, None]
    kc = np.arange(GRID_W)[None, :]
    cstart = np.clip(qc - NA_KW // 2, 0, GRID_W - NA_KW)
    valid = (kc >= cstart) & (kc < cstart + NA_KW)
    cidx = np.clip(kc - qc + NA_KW - 1, 0, 2 * NA_KW - 2)
    onehot = (cidx[None] == np.arange(2 * NA_KW - 1)[:, None, None]).astype(np.float32)
    tab = jnp.einsum("hrd,dqk->hrqk", rpb.astype(F32) * LOG2_E, onehot, precision=lax.Precision.HIGHEST)
    tab = jnp.where(valid, tab, MASK_VALUE)
    return jnp.concatenate([tab[:, :-1], tab[:, 1:]], axis=-1)


def _na_kernel(q_ref, k_ref, v_ref, bias_ref, o_ref, s_ref):
    w = GRID_W
    per = NA_ROWS_PER_STAGE
    n_pairs = q_ref.shape[1]
    n_rows = q_ref.shape[2] // w
    groups_per_pair = n_rows // per
    n_groups = n_pairs * groups_per_pair
    first_head = lax.broadcasted_iota(jnp.int32, (w, V7X_LANES), 1) < NA_HEAD_DIM

    def locate(g, i):
        g = jnp.int32(g)
        pair = lax.div(g, jnp.int32(groups_per_pair))
        r = (g - pair * groups_per_pair) * per + i
        start = jnp.clip(r - NA_KH // 2, 0, n_rows - NA_KH)
        return (pair, pl.ds(pl.multiple_of(r * w, w), w), start - r + NA_KH - 1,
                pl.ds(pl.multiple_of(start * w, w), NA_KH * w))

    def scores(g):
        for i in range(per):
            pair, rows, _, keys = locate(g, i)
            q = q_ref[0, pair, rows, :]
            zero = jnp.zeros_like(q)
            q2 = jnp.concatenate([jnp.where(first_head, q, zero), jnp.where(first_head, zero, q)], axis=0)
            s_ref[i] = lax.dot_general(q2, k_ref[0, pair, keys, :], (((1,), (1,)), ((), ())),
                                       preferred_element_type=F32)

    def attend(g):
        for i in range(per):
            pair, rows, off, keys = locate(g, i)
            probs, denoms = [], []
            for hh in range(NA_HEADS_PER_STEP):
                head = pair * NA_HEADS_PER_STEP + hh
                bias = jnp.concatenate([bias_ref[head, off + j] for j in range(0, NA_KH, 2)], axis=1)
                s = s_ref[i, hh * w:(hh + 1) * w, :] + bias
                p = jnp.exp2(s - jnp.max(s, axis=-1, keepdims=True))
                denoms.append(jnp.sum(p, axis=-1, keepdims=True))
                probs.append(p.astype(BF16))
            o = jnp.dot(jnp.concatenate(probs, axis=0), v_ref[0, pair, keys, :], preferred_element_type=F32)
            o = jnp.where(first_head, o[:w] / denoms[0], o[w:] / denoms[1])
            o_ref[0, pair, rows, :] = o.astype(BF16)

    def steady(g, carry):
        attend(g - 1)
        scores(g)
        return carry

    scores(0)
    lax.fori_loop(1, n_groups, steady, 0)
    attend(n_groups - 1)


def _neighborhood_attention(q, k, v, rpb):
    bsz, n_pairs, t, _ = q.shape
    n_rows = t // GRID_W
    assert t % GRID_W == 0 and n_rows >= NA_KH and n_rows % NA_ROWS_PER_STAGE == 0
    assert n_rows // NA_ROWS_PER_STAGE >= 2 and n_pairs * NA_HEADS_PER_STEP == NA_HEADS
    bias = _na_bias_table(rpb)
    spec = pl.BlockSpec((1, n_pairs, t, V7X_LANES), lambda b: (b, 0, 0, 0))
    stage_rows = NA_HEADS_PER_STEP * GRID_W
    return pl.pallas_call(
        _na_kernel,
        grid=(bsz,),
        in_specs=[spec, spec, spec, _resident(bias.shape)],
        out_specs=spec,
        out_shape=jax.ShapeDtypeStruct(q.shape, BF16),
        scratch_shapes=[pltpu.VMEM((NA_ROWS_PER_STAGE, stage_rows, NA_KH * GRID_W), F32)],
        compiler_params=_params("parallel"),
        name="neighborhood_attention",
    )(q, k, v, bias)


def _outproj_kernel(h_ref, g_ref, yn_ref, sg_ref, gain_ref, na_ref, gtr_ref, gtn_ref,
                    wr_ref, wn_ref, wo_ref, o_ref):
    slab = h_ref.shape[1] // OUTPROJ_ROW_SLABS
    for s in range(OUTPROJ_ROW_SLABS):
        rows = slice(s * slab, (s + 1) * slab)
        gated = sg_ref[0, rows, :].astype(F32) * (yn_ref[0, rows, :].astype(F32) * gain_ref[...])
        ret = jnp.dot(gated.astype(BF16), wr_ref[...], preferred_element_type=F32)
        na_in = jnp.concatenate([na_ref[0, p, rows, :] for p in range(na_ref.shape[1])], axis=1)
        na = jnp.dot(na_in, wn_ref[...], preferred_element_type=F32)
        merged = (jax.nn.sigmoid(gtr_ref[0, rows, :].astype(F32)) * ret
                  + jax.nn.sigmoid(gtn_ref[0, rows, :].astype(F32)) * na)
        out = jnp.dot(merged.astype(BF16), wo_ref[...], preferred_element_type=F32)
        o_ref[0, rows, :] = h_ref[0, rows, :] + g_ref[0, 0] * out


def _outproj(h, mod, yn, silu_g, gn_gain, na, gate_r, gate_n, w_ret_o, w_na_o, w_out):
    bsz, t, d = h.shape
    tm = min(TOKEN_TILE, t)

    def tile(width):
        return pl.BlockSpec((1, tm, width), lambda b, i: (b, i, 0))

    return pl.pallas_call(
        _outproj_kernel,
        grid=(bsz, t // tm),
        in_specs=[tile(d), _mod_spec(5), tile(RET_V_WIDTH), tile(RET_V_WIDTH), _resident((1, RET_V_WIDTH)),
                  pl.BlockSpec((1, na.shape[1], tm, V7X_LANES), lambda b, i: (b, 0, i, 0)), tile(d), tile(d),
                  _resident(w_ret_o.shape), _resident(w_na_o.shape), _resident(w_out.shape)],
        out_specs=tile(d),
        out_shape=jax.ShapeDtypeStruct(h.shape, F32),
        compiler_params=_params("parallel", "parallel"),
        name="mixer_outproj",
    )(h, mod, yn, silu_g, gn_gain.reshape(1, RET_V_WIDTH), na, gate_r, gate_n, w_ret_o, w_na_o, w_out)


def _trunk(x, mods, weights, final_norm):
    assert x.shape[1] % RET_CHUNK == 0 and x.shape[2] == D_MODEL
    h = x
    depth = len(weights)
    for l, wl in enumerate(weights):
        (ffn1_norm, ffn1_wi, ffn1_wo, mix_norm, w_in, decay_logit, gn_gain, w_ret_o, rpb,
         w_na_o, w_out, ffn2_norm, ffn2_wi, ffn2_wo) = wl
        mod = mods[l]
        h = _ffn(h, mod, 0, ffn1_norm, ffn1_wi, ffn1_wo)
        q_r, k_r, v_r, silu_g, q_n, k_n, v_n, gate_r, gate_n = _inproj(h, mod, mix_norm, w_in)
        yn = _retention(q_r, k_r, v_r, decay_logit)
        na = _neighborhood_attention(q_n, k_n, v_n, rpb)
        h = _outproj(h, mod, yn, silu_g, gn_gain, na, gate_r, gate_n, w_ret_o, w_na_o, w_out)
        h = _ffn(h, mod, 6, ffn2_norm, ffn2_wi, ffn2_wo,
                 final_gain=final_norm if l == depth - 1 else None)
    return h


def kernel(x_prompt, x_sample, c_prompt, c_sample, ada_w, ada_b, ffn1_norm, ffn1_wi, ffn1_wo, mix_norm, w_in, ret_decay_logit, ret_gn_gain, w_ret_o, na_rpb, w_na_o, w_out, ffn2_norm, ffn2_wi, ffn2_wo, final_norm):
    depth = ada_w.shape[0]
    nbp = c_prompt.shape[0]
    mods = _adaln(jnp.concatenate([c_prompt, c_sample], axis=0), ada_w, ada_b)
    mods = mods.reshape(depth, mods.shape[1], N_MOD, 1, D_MODEL)
    bf = lambda w: w.astype(BF16)
    weights = [
        (ffn1_norm[l], bf(ffn1_wi[l]), bf(ffn1_wo[l]), mix_norm[l], bf(w_in[l]), ret_decay_logit[l],
         ret_gn_gain[l], bf(w_ret_o[l]), na_rpb[l], bf(w_na_o[l]), bf(w_out[l]),
         ffn2_norm[l], bf(ffn2_wi[l]), bf(ffn2_wo[l]))
        for l in range(depth)
    ]
    y_prompt = _trunk(x_prompt, mods[:, :nbp], weights, final_norm)
    y_sample = _trunk(x_sample, mods[:, nbp:], weights, final_norm)
    return (y_prompt, y_sample)
```

```python
import functools

import numpy as np
import jax
import jax.numpy as jnp
from jax import lax
from jax.experimental import pallas as pl
from jax.experimental.pallas import tpu as pltpu

F32 = jnp.float32
BF16 = jnp.bfloat16

D_MODEL = 1024
GRID_W = 64
RET_HEADS = 4
RET_QK_DIM = 128
RET_V_DIM = 256
RET_CHUNK = 128
RET_QK_WIDTH = RET_HEADS * RET_QK_DIM
RET_V_WIDTH = RET_HEADS * RET_V_DIM
NA_HEADS = 8
NA_HEAD_DIM = 64
NA_WIDTH = NA_HEADS * NA_HEAD_DIM
NA_KH = 8
NA_KW = 16
D_FF = 2816
N_MOD = 9
ROPE_BASE = 10000.0
RMS_EPS = 1e-6
GN_EPS = 1e-5
IN_WIDTHS = (RET_QK_WIDTH, RET_QK_WIDTH, RET_V_WIDTH, RET_V_WIDTH,
             NA_WIDTH, NA_WIDTH, NA_WIDTH, D_MODEL, D_MODEL)
NA_OUTPUTS = (4, 5, 6)
MASK_VALUE = float("-inf")
LOG2_E = 1.4426950408889634
NA_LOGIT_SCALE = NA_HEAD_DIM ** -0.5 * LOG2_E

V7X_LANES = 128
V7X_VMEM_BYTES = 64 * 1024 * 1024
VMEM_LIMIT_BYTES = V7X_VMEM_BYTES * 7 // 8

TOKEN_TILE = 1024
FFN_CHUNK = 256
PROJ_CHUNK = 512
NA_HEADS_PER_STEP = V7X_LANES // NA_HEAD_DIM
NA_ROWS_PER_STAGE = 4
RET_BLOCK = 256
RET_BLOCKS_PER_STEP = 16
FFN_ROW_SLABS = 8
FFN_FINAL_ROW_SLABS = 4
INPROJ_ROW_SLABS = 4
OUTPROJ_ROW_SLABS = 2
NA_PIPELINE_UNROLL = 3


def _params(*semantics):
    return pltpu.CompilerParams(dimension_semantics=semantics, vmem_limit_bytes=VMEM_LIMIT_BYTES)


def _resident(shape):
    return pl.BlockSpec(shape, lambda *_: (0,) * len(shape), pipeline_mode=pl.Buffered(1))


def _silu(x):
    return x * jax.nn.sigmoid(x)


def _modulated_rms(x, gain, shift, scale):
    y = x * lax.rsqrt(jnp.mean(x * x, axis=-1, keepdims=True) + RMS_EPS) * gain
    return y * (1.0 + scale) + shift


def _largest_divisor(n, limit):
    return max(d for d in range(1, limit + 1) if n % d == 0)


def _mod_kernel(c_ref, w_ref, b_ref, o_ref):
    a = _silu(c_ref[...]).astype(BF16)
    o_ref[0] = jnp.dot(a, w_ref[0].astype(BF16), preferred_element_type=F32) + b_ref[0]


def _adaln(c, ada_w, ada_b):
    nb, d = c.shape
    depth, _, width = ada_w.shape
    tn = d
    return pl.pallas_call(
        _mod_kernel,
        grid=(depth, width // tn),
        in_specs=[
            pl.BlockSpec((nb, d), lambda l, j: (0, 0)),
            pl.BlockSpec((1, d, tn), lambda l, j: (l, 0, j)),
            pl.BlockSpec((1, 1, tn), lambda l, j: (l, 0, j)),
        ],
        out_specs=pl.BlockSpec((1, nb, tn), lambda l, j: (l, 0, j)),
        out_shape=jax.ShapeDtypeStruct((depth, nb, width), F32),
        compiler_params=_params("parallel", "parallel"),
        name="adaln_mod",
    )(c, ada_w, ada_b.reshape(depth, 1, width))


def _mod_spec(k):
    return pl.BlockSpec((1, 1, 1, D_MODEL), lambda b, t: (b, k, 0, 0))


def _ffn_kernel(h_ref, gain_ref, sh_ref, sc_ref, g_ref, wi_ref, wo_ref, *rest, final):
    if final:
        fgain_ref, o_ref, mid_ref = rest
    else:
        o_ref, mid_ref = rest
    n_slabs = FFN_FINAL_ROW_SLABS if final else FFN_ROW_SLABS
    slab = h_ref.shape[1] // n_slabs
    for s in range(n_slabs):
        rows = slice(s * slab, (s + 1) * slab)
        x = h_ref[0, rows, :]
        nb = _modulated_rms(x, gain_ref[...], sh_ref[0, 0], sc_ref[0, 0]).astype(BF16)
        for j in range(D_FF // FFN_CHUNK):
            lo = j * FFN_CHUNK
            a = jnp.dot(nb, wi_ref[:, lo:lo + FFN_CHUNK], preferred_element_type=F32)
            b = jnp.dot(nb, wi_ref[:, D_FF + lo:D_FF + lo + FFN_CHUNK], preferred_element_type=F32)
            mid_ref[rows, lo:lo + FFN_CHUNK] = (_silu(a) * b).astype(BF16)
        y = jnp.dot(mid_ref[rows, :], wo_ref[...], preferred_element_type=F32)
        out = x + (0.5 * g_ref[0, 0]) * y
        if final:
            out = out * lax.rsqrt(jnp.mean(out * out, axis=-1, keepdims=True) + RMS_EPS) * fgain_ref[...]
        o_ref[0, rows, :] = out


def _ffn(h, mod, k0, gain, wi, wo, final_gain=None):
    bsz, t, d = h.shape
    tm = min(TOKEN_TILE, t)
    final = final_gain is not None
    assert t % tm == 0 and tm % (FFN_FINAL_ROW_SLABS if final else FFN_ROW_SLABS) == 0
    assert D_FF % FFN_CHUNK == 0
    tile = pl.BlockSpec((1, tm, d), lambda b, i: (b, i, 0))
    in_specs = [tile, _resident((1, d)), _mod_spec(k0), _mod_spec(k0 + 1), _mod_spec(k0 + 2),
                _resident(wi.shape), _resident(wo.shape)]
    args = [h, gain.reshape(1, d), mod, mod, mod, wi, wo]
    if final:
        in_specs.append(_resident((1, d)))
        args.append(final_gain.reshape(1, d))
    return pl.pallas_call(
        functools.partial(_ffn_kernel, final=final),
        grid=(bsz, t // tm),
        in_specs=in_specs,
        out_specs=tile,
        out_shape=jax.ShapeDtypeStruct(h.shape, F32),
        scratch_shapes=[pltpu.VMEM((tm, D_FF), BF16)],
        compiler_params=_params("parallel", "parallel"),
        name="ffn_final" if final else "ffn",
    )(*args)


def _rotary(y, cos, sin_signed):
    heads = []
    for hd in range(y.shape[1] // RET_QK_DIM):
        yh = y[:, hd * RET_QK_DIM:(hd + 1) * RET_QK_DIM]
        heads.append(yh * cos + pltpu.roll(yh, RET_QK_DIM // 2, axis=1) * sin_signed)
    return jnp.concatenate(heads, axis=1)


def _inproj_kernel(h_ref, gain_ref, sh_ref, sc_ref, cos_ref, sin_ref, w_ref,
                   qr_ref, kr_ref, vr_ref, gr_ref, qn_ref, kn_ref, vn_ref, gtr_ref, gtn_ref):
    outs = (qr_ref, kr_ref, vr_ref, gr_ref, qn_ref, kn_ref, vn_ref, gtr_ref, gtn_ref)
    slab = h_ref.shape[1] // INPROJ_ROW_SLABS
    for s in range(INPROJ_ROW_SLABS):
        rows = slice(s * slab, (s + 1) * slab)
        nb = _modulated_rms(h_ref[0, rows, :], gain_ref[...], sh_ref[0, 0], sc_ref[0, 0]).astype(BF16)
        cos = cos_ref[rows, :]
        sin_signed = sin_ref[rows, :]
        col = 0
        for idx, (o_ref, width) in enumerate(zip(outs, IN_WIDTHS)):
            for lo in range(0, width, PROJ_CHUNK):
                y = jnp.dot(nb, w_ref[:, col + lo:col + lo + PROJ_CHUNK], preferred_element_type=F32)
                if idx == 0:
                    y = _rotary(y, cos, sin_signed)
                elif idx == 1:
                    y = _rotary(y, cos, sin_signed) * (RET_QK_DIM ** -0.5)
                elif idx == 3:
                    y = _silu(y)
                elif idx == 4:
                    y = y * NA_LOGIT_SCALE
                if idx in NA_OUTPUTS:
                    for j in range(PROJ_CHUNK // V7X_LANES):
                        o_ref[0, lo // V7X_LANES + j, rows, :] = (
                            y[:, j * V7X_LANES:(j + 1) * V7X_LANES].astype(BF16))
                else:
                    o_ref[0, rows, lo:lo + PROJ_CHUNK] = y.astype(BF16)
            col += width


def _rope_tables(t):
    half = RET_QK_DIM // 2
    inv = ROPE_BASE ** (-np.arange(half, dtype=np.float64) / half)
    ang = np.arange(t, dtype=np.float64)[:, None] * inv[None, :]
    cos = np.concatenate([np.cos(ang), np.cos(ang)], axis=1)
    sin_signed = np.concatenate([-np.sin(ang), np.sin(ang)], axis=1)
    return jnp.asarray(cos, F32), jnp.asarray(sin_signed, F32)


def _inproj(h, mod, gain, w_in):
    bsz, t, d = h.shape
    tm = min(TOKEN_TILE, t)
    assert t % tm == 0 and tm % INPROJ_ROW_SLABS == 0 and w_in.shape[1] == sum(IN_WIDTHS)
    cos, sin_signed = _rope_tables(t)
    table = pl.BlockSpec((tm, RET_QK_DIM), lambda b, i: (i, 0))
    out_specs, out_shape = [], []
    for idx, w in enumerate(IN_WIDTHS):
        if idx in NA_OUTPUTS:
            pairs = w // V7X_LANES
            out_specs.append(pl.BlockSpec((1, pairs, tm, V7X_LANES), lambda b, i: (b, 0, i, 0)))
            out_shape.append(jax.ShapeDtypeStruct((bsz, pairs, t, V7X_LANES), BF16))
        else:
            out_specs.append(pl.BlockSpec((1, tm, w), lambda b, i: (b, i, 0)))
            out_shape.append(jax.ShapeDtypeStruct((bsz, t, w), BF16))
    return pl.pallas_call(
        _inproj_kernel,
        grid=(bsz, t // tm),
        in_specs=[pl.BlockSpec((1, tm, d), lambda b, i: (b, i, 0)), _resident((1, d)),
                  _mod_spec(3), _mod_spec(4), table, table, _resident(w_in.shape)],
        out_specs=out_specs,
        out_shape=out_shape,
        compiler_params=_params("parallel", "parallel"),
        name="mixer_inproj",
    )(h, gain.reshape(1, d), mod, mod, cos, sin_signed, w_in)


def _log_sigmoid(x):
    return jnp.minimum(x, 0.0) - jnp.log1p(jnp.exp(-jnp.abs(x)))


def _retention_kernel(dl_ref, q_ref, k_ref, v_ref, o_ref,
                      inc_ref, pre_ref, state_ref, decay_ref, xi_ref, zeta_ref, *, group):
    c, dk = RET_BLOCK, RET_QK_DIM
    n_chunks = q_ref.shape[1] // c
    hd = pl.program_id(1)

    def log_gamma(direction, shape):
        return _log_sigmoid(jnp.full(shape, dl_ref[direction, hd], F32))

    def position(shape):
        return lax.broadcasted_iota(jnp.int32, shape, 0).astype(F32)

    diff = (lax.broadcasted_iota(jnp.int32, (c, c), 0) - lax.broadcasted_iota(jnp.int32, (c, c), 1)).astype(F32)
    decay_ref[...] = jnp.exp(jnp.where(diff >= 0, diff * log_gamma(0, (c, c)), -diff * log_gamma(1, (c, c))))
    k_shape = (c, dk)
    xi_ref[0] = jnp.exp((position(k_shape) + 1.0) * log_gamma(0, k_shape))
    xi_ref[1] = jnp.exp((c - position(k_shape)) * log_gamma(1, k_shape))
    zeta_ref[0] = jnp.exp((c - 1.0 - position(k_shape)) * log_gamma(0, k_shape))
    zeta_ref[1] = jnp.exp(position(k_shape) * log_gamma(1, k_shape))
    chunk_decay = [jnp.exp(c * log_gamma(d, (1, RET_V_DIM))) for d in range(2)]

    def chunk_rows(i):
        return pl.ds(pl.multiple_of(i * c, c), c)

    def increments(it, carry):
        for j in range(group):
            i = it * group + j
            k = k_ref[0, chunk_rows(i), :].astype(F32)
            kz = jnp.concatenate([k * zeta_ref[0], k * zeta_ref[1]], axis=1).astype(BF16)
            inc_ref[i] = lax.dot_general(kz, v_ref[0, chunk_rows(i), :], (((0,), (0,)), ((), ())),
                                         preferred_element_type=F32)
        return carry

    def scan(j, carry):
        for d, i in ((0, j), (1, n_chunks - 1 - j)):
            half = slice(d * dk, (d + 1) * dk)
            state = state_ref[d]
            pre_ref[i, half, :] = state.astype(BF16)
            state_ref[d] = state * chunk_decay[d] + inc_ref[i, half, :]
        return carry

    def outputs(it, carry):
        for j in range(group):
            i = it * group + j
            rows = chunk_rows(i)
            q = q_ref[0, rows, :]
            v = v_ref[0, rows, :]
            s = lax.dot_general(q, k_ref[0, rows, :], (((1,), (1,)), ((), ())), preferred_element_type=F32)
            qf = q.astype(F32)
            qx = jnp.concatenate([qf * xi_ref[0], qf * xi_ref[1]], axis=1).astype(BF16)
            y = (jnp.dot((s * decay_ref[...]).astype(BF16), v, preferred_element_type=F32)
                 + jnp.dot(qx, pre_ref[i], preferred_element_type=F32))
            yc = y - jnp.mean(y, axis=-1, keepdims=True)
            var = jnp.mean(yc * yc, axis=-1, keepdims=True)
            o_ref[0, rows, :] = (yc * lax.rsqrt(var + GN_EPS)).astype(BF16)
        return carry

    lax.fori_loop(0, n_chunks // group, increments, 0)
    state_ref[...] = jnp.zeros_like(state_ref)
    lax.fori_loop(0, n_chunks, scan, 0)
    lax.fori_loop(0, n_chunks // group, outputs, 0)


def _retention(q, k, v, decay_logit):
    bsz, t, _ = q.shape
    c = RET_BLOCK
    assert t % c == 0
    n_chunks = t // c
    qk_spec = pl.BlockSpec((1, t, RET_QK_DIM), lambda b, h: (b, 0, h))
    v_spec = pl.BlockSpec((1, t, RET_V_DIM), lambda b, h: (b, 0, h))
    return pl.pallas_call(
        functools.partial(_retention_kernel, group=_largest_divisor(n_chunks, RET_BLOCKS_PER_STEP)),
        grid=(bsz, RET_HEADS),
        in_specs=[pl.BlockSpec(memory_space=pltpu.SMEM), qk_spec, qk_spec, v_spec],
        out_specs=v_spec,
        out_shape=jax.ShapeDtypeStruct((bsz, t, RET_V_WIDTH), BF16),
        scratch_shapes=[pltpu.VMEM((n_chunks, 2 * RET_QK_DIM, RET_V_DIM), F32),
                        pltpu.VMEM((n_chunks, 2 * RET_QK_DIM, RET_V_DIM), BF16),
                        pltpu.VMEM((2, RET_QK_DIM, RET_V_DIM), F32),
                        pltpu.VMEM((c, c), F32),
                        pltpu.VMEM((2, c, RET_QK_DIM), F32),
                        pltpu.VMEM((2, c, RET_QK_DIM), F32)],
        compiler_params=_params("parallel", "parallel"),
        name="retention",
    )(decay_logit, q, k, v)


def _na_bias_table(rpb):
    qc = np.arange(GRID_W)[:, None]
    kc = np.arange(GRID_W)[None, :]
    cstart = np.clip(qc - NA_KW // 2, 0, GRID_W - NA_KW)
    valid = (kc >= cstart) & (kc < cstart + NA_KW)
    cidx = np.clip(kc - qc + NA_KW - 1, 0, 2 * NA_KW - 2)
    onehot = (cidx[None] == np.arange(2 * NA_KW - 1)[:, None, None]).astype(np.float32)
    tab = jnp.einsum("hrd,dqk->hrqk", rpb.astype(F32) * LOG2_E, onehot, precision=lax.Precision.HIGHEST)
    tab = jnp.where(valid, tab, MASK_VALUE)
    return jnp.concatenate([tab[:, :-1], tab[:, 1:]], axis=-1)


def _na_kernel(q_ref, k_ref, v_ref, bias_ref, o_ref, s_ref):
    w = GRID_W
    per = NA_ROWS_PER_STAGE
    n_pairs = q_ref.shape[1]
    n_rows = q_ref.shape[2] // w
    groups_per_pair = n_rows // per
    n_groups = n_pairs * groups_per_pair
    first_head = lax.broadcasted_iota(jnp.int32, (w, V7X_LANES), 1) < NA_HEAD_DIM

    def locate(g, i):
        g = jnp.int32(g)
        pair = lax.div(g, jnp.int32(groups_per_pair))
        r = (g - pair * groups_per_pair) * per + i
        start = jnp.clip(r - NA_KH // 2, 0, n_rows - NA_KH)
        return (pair, pl.ds(pl.multiple_of(r * w, w), w), start - r + NA_KH - 1,
                pl.ds(pl.multiple_of(start * w, w), NA_KH * w))

    def scores(g):
        for i in range(per):
            pair, rows, _, keys = locate(g, i)
            q = q_ref[0, pair, rows, :]
            zero = jnp.zeros_like(q)
            q2 = jnp.concatenate([jnp.where(first_head, q, zero), jnp.where(first_head, zero, q)], axis=0)
            s_ref[i] = lax.dot_general(q2, k_ref[0, pair, keys, :], (((1,), (1,)), ((), ())),
                                       preferred_element_type=F32)

    def attend(g):
        for i in range(per):
            pair, rows, off, keys = locate(g, i)
            probs, denoms = [], []
            for hh in range(NA_HEADS_PER_STEP):
                head = pair * NA_HEADS_PER_STEP + hh
                bias = jnp.concatenate([bias_ref[head, off + j] for j in range(0, NA_KH, 2)], axis=1)
                s = s_ref[i, hh * w:(hh + 1) * w, :] + bias
                p = jnp.exp2(s - jnp.max(s, axis=-1, keepdims=True))
                denoms.append(jnp.sum(p, axis=-1, keepdims=True))
                probs.append(p.astype(BF16))
            o = jnp.dot(jnp.concatenate(probs, axis=0), v_ref[0, pair, keys, :], preferred_element_type=F32)
            o = jnp.where(first_head, o[:w] / denoms[0], o[w:] / denoms[1])
            o_ref[0, pair, rows, :] = o.astype(BF16)

    def steady(g, carry):
        attend(g - 1)
        scores(g)
        return carry

    scores(0)
    lax.fori_loop(1, n_groups, steady, 0, unroll=NA_PIPELINE_UNROLL)
    attend(n_groups - 1)


def _neighborhood_attention(q, k, v, rpb):
    bsz, n_pairs, t, _ = q.shape
    n_rows = t // GRID_W
    assert t % GRID_W == 0 and n_rows >= NA_KH and n_rows % NA_ROWS_PER_STAGE == 0
    assert n_rows // NA_ROWS_PER_STAGE >= 2 and n_pairs * NA_HEADS_PER_STEP == NA_HEADS
    bias = _na_bias_table(rpb)
    spec = pl.BlockSpec((1, n_pairs, t, V7X_LANES), lambda b: (b, 0, 0, 0))
    stage_rows = NA_HEADS_PER_STEP * GRID_W
    return pl.pallas_call(
        _na_kernel,
        grid=(bsz,),
        in_specs=[spec, spec, spec, _resident(bias.shape)],
        out_specs=spec,
        out_shape=jax.ShapeDtypeStruct(q.shape, BF16),
        scratch_shapes=[pltpu.VMEM((NA_ROWS_PER_STAGE, stage_rows, NA_KH * GRID_W), F32)],
        compiler_params=_params("parallel"),
        name="neighborhood_attention",
    )(q, k, v, bias)


def _outproj_kernel(h_ref, g_ref, yn_ref, sg_ref, gain_ref, na_ref, gtr_ref, gtn_ref,
                    wr_ref, wn_ref, wo_ref, o_ref):
    slab = h_ref.shape[1] // OUTPROJ_ROW_SLABS
    for s in range(OUTPROJ_ROW_SLABS):
        rows = slice(s * slab, (s + 1) * slab)
        gated = sg_ref[0, rows, :].astype(F32) * (yn_ref[0, rows, :].astype(F32) * gain_ref[...])
        ret = jnp.dot(gated.astype(BF16), wr_ref[...], preferred_element_type=F32)
        na_in = jnp.concatenate([na_ref[0, p, rows, :] for p in range(na_ref.shape[1])], axis=1)
        na = jnp.dot(na_in, wn_ref[...], preferred_element_type=F32)
        merged = (jax.nn.sigmoid(gtr_ref[0, rows, :].astype(F32)) * ret
                  + jax.nn.sigmoid(gtn_ref[0, rows, :].astype(F32)) * na)
        out = jnp.dot(merged.astype(BF16), wo_ref[...], preferred_element_type=F32)
        o_ref[0, rows, :] = h_ref[0, rows, :] + g_ref[0, 0] * out


def _outproj(h, mod, yn, silu_g, gn_gain, na, gate_r, gate_n, w_ret_o, w_na_o, w_out):
    bsz, t, d = h.shape
    tm = min(TOKEN_TILE, t)
    assert t % tm == 0 and tm % OUTPROJ_ROW_SLABS == 0

    def tile(width):
        return pl.BlockSpec((1, tm, width), lambda b, i: (b, i, 0))

    return pl.pallas_call(
        _outproj_kernel,
        grid=(bsz, t // tm),
        in_specs=[tile(d), _mod_spec(5), tile(RET_V_WIDTH), tile(RET_V_WIDTH), _resident((1, RET_V_WIDTH)),
                  pl.BlockSpec((1, na.shape[1], tm, V7X_LANES), lambda b, i: (b, 0, i, 0)), tile(d), tile(d),
                  _resident(w_ret_o.shape), _resident(w_na_o.shape), _resident(w_out.shape)],
        out_specs=tile(d),
        out_shape=jax.ShapeDtypeStruct(h.shape, F32),
        compiler_params=_params("parallel", "parallel"),
        name="mixer_outproj",
    )(h, mod, yn, silu_g, gn_gain.reshape(1, RET_V_WIDTH), na, gate_r, gate_n, w_ret_o, w_na_o, w_out)


def _trunk(x, mods, weights, final_norm):
    assert x.shape[1] % RET_CHUNK == 0 and x.shape[2] == D_MODEL
    h = x
    depth = len(weights)
    for l, wl in enumerate(weights):
        (ffn1_norm, ffn1_wi, ffn1_wo, mix_norm, w_in, decay_logit, gn_gain, w_ret_o, rpb,
         w_na_o, w_out, ffn2_norm, ffn2_wi, ffn2_wo) = wl
        mod = mods[l]
        h = _ffn(h, mod, 0, ffn1_norm, ffn1_wi, ffn1_wo)
        q_r, k_r, v_r, silu_g, q_n, k_n, v_n, gate_r, gate_n = _inproj(h, mod, mix_norm, w_in)
        yn = _retention(q_r, k_r, v_r, decay_logit)
        na = _neighborhood_attention(q_n, k_n, v_n, rpb)
        h = _outproj(h, mod, yn, silu_g, gn_gain, na, gate_r, gate_n, w_ret_o, w_na_o, w_out)
        h = _ffn(h, mod, 6, ffn2_norm, ffn2_wi, ffn2_wo,
                 final_gain=final_norm if l == depth - 1 else None)
    return h


def kernel(x_prompt, x_sample, c_prompt, c_sample, ada_w, ada_b, ffn1_norm, ffn1_wi, ffn1_wo, mix_norm, w_in, ret_decay_logit, ret_gn_gain, w_ret_o, na_rpb, w_na_o, w_out, ffn2_norm, ffn2_wi, ffn2_wo, final_norm):
    depth = ada_w.shape[0]
    nbp = c_prompt.shape[0]
    mods = _adaln(jnp.concatenate([c_prompt, c_sample], axis=0), ada_w, ada_b)
    mods = mods.reshape(depth, mods.shape[1], N_MOD, 1, D_MODEL)
    bf = lambda w: w.astype(BF16)
    weights = [
        (ffn1_norm[l], bf(ffn1_wi[l]), bf(ffn1_wo[l]), mix_norm[l], bf(w_in[l]), ret_decay_logit[l],
         ret_gn_gain[l], bf(w_ret_o[l]), na_rpb[l], bf(w_na_o[l]), bf(w_out[l]),
         ffn2_norm[l], bf(ffn2_wi[l]), bf(ffn2_wo[l]))
        for l in range(depth)
    ]
    y_prompt = _trunk(x_prompt, mods[:, :nbp], weights, final_norm)
    y_sample = _trunk(x_sample, mods[:, nbp:], weights, final_norm)
    return (y_prompt, y_sample)
```

```python
import functools

import numpy as np
import jax
import jax.numpy as jnp
from jax import lax
from jax.experimental import pallas as pl
from jax.experimental.pallas import tpu as pltpu

F32 = jnp.float32
BF16 = jnp.bfloat16

D_MODEL = 1024
GRID_W = 64
RET_HEADS = 4
RET_QK_DIM = 128
RET_V_DIM = 256
RET_CHUNK = 128
RET_QK_WIDTH = RET_HEADS * RET_QK_DIM
RET_V_WIDTH = RET_HEADS * RET_V_DIM
NA_HEADS = 8
NA_HEAD_DIM = 64
NA_WIDTH = NA_HEADS * NA_HEAD_DIM
NA_KH = 8
NA_KW = 16
D_FF = 2816
N_MOD = 9
ROPE_BASE = 10000.0
RMS_EPS = 1e-6
GN_EPS = 1e-5
IN_WIDTHS = (RET_QK_WIDTH, RET_QK_WIDTH, RET_V_WIDTH, RET_V_WIDTH,
             NA_WIDTH, NA_WIDTH, NA_WIDTH, D_MODEL, D_MODEL)
NA_OUTPUTS = (4, 5, 6)
MASK_VALUE = float("-inf")
LOG2_E = 1.4426950408889634
NA_LOGIT_SCALE = NA_HEAD_DIM ** -0.5 * LOG2_E

V7X_LANES = 128
V7X_VMEM_BYTES = 64 * 1024 * 1024
VMEM_LIMIT_BYTES = V7X_VMEM_BYTES * 7 // 8

TOKEN_TILE = 1024
FFN_CHUNK = 256
PROJ_CHUNK = 512
NA_HEADS_PER_STEP = V7X_LANES // NA_HEAD_DIM
NA_ROWS_PER_STAGE = 4
RET_BLOCK = 256
RET_BLOCKS_PER_STEP = 16
FFN_ROW_SLABS = 8
FFN_FINAL_ROW_SLABS = 4
INPROJ_ROW_SLABS = 4
OUTPROJ_ROW_SLABS = 2
NA_PIPELINE_UNROLL = 3


def _params(*semantics):
    return pltpu.CompilerParams(dimension_semantics=semantics, vmem_limit_bytes=VMEM_LIMIT_BYTES)


def _resident(shape):
    return pl.BlockSpec(shape, lambda *_: (0,) * len(shape), pipeline_mode=pl.Buffered(1))


def _silu(x):
    return x * jax.nn.sigmoid(x)


def _modulated_rms(x, gain, shift, scale):
    y = x * lax.rsqrt(jnp.mean(x * x, axis=-1, keepdims=True) + RMS_EPS) * gain
    return y * (1.0 + scale) + shift


def _largest_divisor(n, limit):
    return max(d for d in range(1, limit + 1) if n % d == 0)


def _mod_kernel(c_ref, w_ref, b_ref, o_ref):
    a = _silu(c_ref[...]).astype(BF16)
    o_ref[0] = jnp.dot(a, w_ref[0].astype(BF16), preferred_element_type=F32) + b_ref[0]


def _adaln(c, ada_w, ada_b):
    nb, d = c.shape
    depth, _, width = ada_w.shape
    tn = d
    return pl.pallas_call(
        _mod_kernel,
        grid=(depth, width // tn),
        in_specs=[
            pl.BlockSpec((nb, d), lambda l, j: (0, 0)),
            pl.BlockSpec((1, d, tn), lambda l, j: (l, 0, j)),
            pl.BlockSpec((1, 1, tn), lambda l, j: (l, 0, j)),
        ],
        out_specs=pl.BlockSpec((1, nb, tn), lambda l, j: (l, 0, j)),
        out_shape=jax.ShapeDtypeStruct((depth, nb, width), F32),
        compiler_params=_params("parallel", "parallel"),
        name="adaln_mod",
    )(c, ada_w, ada_b.reshape(depth, 1, width))


def _mod_spec(k):
    return pl.BlockSpec((1, 1, 1, D_MODEL), lambda b, t: (b, k, 0, 0))


def _ffn_kernel(h_ref, gain_ref, sh_ref, sc_ref, g_ref, wi_ref, wo_ref, *rest, final):
    if final:
        fgain_ref, o_ref, mid_ref = rest
    else:
        o_ref, mid_ref = rest
    n_slabs = FFN_FINAL_ROW_SLABS if final else FFN_ROW_SLABS
    slab = h_ref.shape[1] // n_slabs
    for s in range(n_slabs):
        rows = slice(s * slab, (s + 1) * slab)
        x = h_ref[0, rows, :]
        nb = _modulated_rms(x, gain_ref[...], sh_ref[0, 0], sc_ref[0, 0]).astype(BF16)
        for j in range(D_FF // FFN_CHUNK):
            lo = j * FFN_CHUNK
            a = jnp.dot(nb, wi_ref[:, lo:lo + FFN_CHUNK], preferred_element_type=F32)
            b = jnp.dot(nb, wi_ref[:, D_FF + lo:D_FF + lo + FFN_CHUNK], preferred_element_type=F32)
            mid_ref[rows, lo:lo + FFN_CHUNK] = (_silu(a) * b).astype(BF16)
        y = jnp.dot(mid_ref[rows, :], wo_ref[...], preferred_element_type=F32)
        out = x + (0.5 * g_ref[0, 0]) * y
        if final:
            out = out * lax.rsqrt(jnp.mean(out * out, axis=-1, keepdims=True) + RMS_EPS) * fgain_ref[...]
        o_ref[0, rows, :] = out


def _ffn(h, mod, k0, gain, wi, wo, final_gain=None):
    bsz, t, d = h.shape
    tm = min(TOKEN_TILE, t)
    final = final_gain is not None
    assert t % tm == 0 and tm % (FFN_FINAL_ROW_SLABS if final else FFN_ROW_SLABS) == 0
    assert D_FF % FFN_CHUNK == 0
    tile = pl.BlockSpec((1, tm, d), lambda b, i: (b, i, 0))
    in_specs = [tile, _resident((1, d)), _mod_spec(k0), _mod_spec(k0 + 1), _mod_spec(k0 + 2),
                _resident(wi.shape), _resident(wo.shape)]
    args = [h, gain.reshape(1, d), mod, mod, mod, wi, wo]
    if final:
        in_specs.append(_resident((1, d)))
        args.append(final_gain.reshape(1, d))
    return pl.pallas_call(
        functools.partial(_ffn_kernel, final=final),
        grid=(bsz, t // tm),
        in_specs=in_specs,
        out_specs=tile,
        out_shape=jax.ShapeDtypeStruct(h.shape, F32),
        scratch_shapes=[pltpu.VMEM((tm, D_FF), BF16)],
        compiler_params=_params("parallel", "parallel"),
        name="ffn_final" if final else "ffn",
    )(*args)


def _rotary(y, cos, sin_signed):
    heads = []
    for hd in range(y.shape[1] // RET_QK_DIM):
        yh = y[:, hd * RET_QK_DIM:(hd + 1) * RET_QK_DIM]
        heads.append(yh * cos + pltpu.roll(yh, RET_QK_DIM // 2, axis=1) * sin_signed)
    return jnp.concatenate(heads, axis=1)


def _inproj_kernel(h_ref, gain_ref, sh_ref, sc_ref, cos_ref, sin_ref, w_ref,
                   qr_ref, kr_ref, vr_ref, gr_ref, qn_ref, kn_ref, vn_ref, gtr_ref, gtn_ref):
    outs = (qr_ref, kr_ref, vr_ref, gr_ref, qn_ref, kn_ref, vn_ref, gtr_ref, gtn_ref)
    slab = h_ref.shape[1] // INPROJ_ROW_SLABS
    for s in range(INPROJ_ROW_SLABS):
        rows = slice(s * slab, (s + 1) * slab)
        nb = _modulated_rms(h_ref[0, rows, :], gain_ref[...], sh_ref[0, 0], sc_ref[0, 0]).astype(BF16)
        cos = cos_ref[rows, :]
        sin_signed = sin_ref[rows, :]
        col = 0
        for idx, (o_ref, width) in enumerate(zip(outs, IN_WIDTHS)):
            for lo in range(0, width, PROJ_CHUNK):
                y = jnp.dot(nb, w_ref[:, col + lo:col + lo + PROJ_CHUNK], preferred_element_type=F32)
                if idx == 0:
                    y = _rotary(y, cos, sin_signed)
                elif idx == 1:
                    y = _rotary(y, cos, sin_signed) * (RET_QK_DIM ** -0.5)
                elif idx == 3:
                    y = _silu(y)
                elif idx == 4:
                    y = y * NA_LOGIT_SCALE
                if idx in NA_OUTPUTS:
                    for j in range(PROJ_CHUNK // V7X_LANES):
                        o_ref[0, lo // V7X_LANES + j, rows, :] = (
                            y[:, j * V7X_LANES:(j + 1) * V7X_LANES].astype(BF16))
                else:
                    o_ref[0, rows, lo:lo + PROJ_CHUNK] = y.astype(BF16)
            col += width


def _rope_tables(t):
    half = RET_QK_DIM // 2
    inv = ROPE_BASE ** (-np.arange(half, dtype=np.float64) / half)
    ang = np.arange(t, dtype=np.float64)[:, None] * inv[None, :]
    cos = np.concatenate([np.cos(ang), np.cos(ang)], axis=1)
    sin_signed = np.concatenate([-np.sin(ang), np.sin(ang)], axis=1)
    return jnp.asarray(cos, F32), jnp.asarray(sin_signed, F32)


def _inproj(h, mod, gain, w_in):
    bsz, t, d = h.shape
    tm = min(TOKEN_TILE, t)
    assert t % tm == 0 and tm % INPROJ_ROW_SLABS == 0 and w_in.shape[1] == sum(IN_WIDTHS)
    cos, sin_signed = _rope_tables(t)
    table = pl.BlockSpec((tm, RET_QK_DIM), lambda b, i: (i, 0))
    out_specs, out_shape = [], []
    for idx, w in enumerate(IN_WIDTHS):
        if idx in NA_OUTPUTS:
            pairs = w // V7X_LANES
            out_specs.append(pl.BlockSpec((1, pairs, tm, V7X_LANES), lambda b, i: (b, 0, i, 0)))
            out_shape.append(jax.ShapeDtypeStruct((bsz, pairs, t, V7X_LANES), BF16))
        else:
            out_specs.append(pl.BlockSpec((1, tm, w), lambda b, i: (b, i, 0)))
            out_shape.append(jax.ShapeDtypeStruct((bsz, t, w), BF16))
    return pl.pallas_call(
        _inproj_kernel,
        grid=(bsz, t // tm),
        in_specs=[pl.BlockSpec((1, tm, d), lambda b, i: (b, i, 0)), _resident((1, d)),
                  _mod_spec(3), _mod_spec(4), table, table, _resident(w_in.shape)],
        out_specs=out_specs,
        out_shape=out_shape,
        compiler_params=_params("parallel", "parallel"),
        name="mixer_inproj",
    )(h, gain.reshape(1, d), mod, mod, cos, sin_signed, w_in)


def _log_sigmoid(x):
    return jnp.minimum(x, 0.0) - jnp.log1p(jnp.exp(-jnp.abs(x)))


def _retention_kernel(dl_ref, q_ref, k_ref, v_ref, o_ref,
                      inc_ref, pre_ref, state_ref, decay_ref, xi_ref, zeta_ref, *, group):
    c, dk = RET_BLOCK, RET_QK_DIM
    n_chunks = q_ref.shape[1] // c
    hd = pl.program_id(1)

    def log_gamma(direction, shape):
        return _log_sigmoid(jnp.full(shape, dl_ref[direction, hd], F32))

    def position(shape):
        return lax.broadcasted_iota(jnp.int32, shape, 0).astype(F32)

    diff = (lax.broadcasted_iota(jnp.int32, (c, c), 0) - lax.broadcasted_iota(jnp.int32, (c, c), 1)).astype(F32)
    decay_ref[...] = jnp.exp(jnp.where(diff >= 0, diff * log_gamma(0, (c, c)), -diff * log_gamma(1, (c, c))))
    k_shape = (c, dk)
    xi_ref[0] = jnp.exp((position(k_shape) + 1.0) * log_gamma(0, k_shape))
    xi_ref[1] = jnp.exp((c - position(k_shape)) * log_gamma(1, k_shape))
    zeta_ref[0] = jnp.exp((c - 1.0 - position(k_shape)) * log_gamma(0, k_shape))
    zeta_ref[1] = jnp.exp(position(k_shape) * log_gamma(1, k_shape))
    chunk_decay = [jnp.exp(c * log_gamma(d, (1, RET_V_DIM))) for d in range(2)]

    def chunk_rows(i):
        return pl.ds(pl.multiple_of(i * c, c), c)

    def increments(it, carry):
        for j in range(group):
            i = it * group + j
            k = k_ref[0, chunk_rows(i), :].astype(F32)
            kz = jnp.concatenate([k * zeta_ref[0], k * zeta_ref[1]], axis=1).astype(BF16)
            inc_ref[i] = lax.dot_general(kz, v_ref[0, chunk_rows(i), :], (((0,), (0,)), ((), ())),
                                         preferred_element_type=F32)
        return carry

    def scan(j, carry):
        for d, i in ((0, j), (1, n_chunks - 1 - j)):
            half = slice(d * dk, (d + 1) * dk)
            state = state_ref[d]
            pre_ref[i, half, :] = state.astype(BF16)
            state_ref[d] = state * chunk_decay[d] + inc_ref[i, half, :]
        return carry

    def outputs(it, carry):
        for j in range(group):
            i = it * group + j
            rows = chunk_rows(i)
            q = q_ref[0, rows, :]
            v = v_ref[0, rows, :]
            s = lax.dot_general(q, k_ref[0, rows, :], (((1,), (1,)), ((), ())), preferred_element_type=F32)
            qf = q.astype(F32)
            qx = jnp.concatenate([qf * xi_ref[0], qf * xi_ref[1]], axis=1).astype(BF16)
            y = (jnp.dot((s * decay_ref[...]).astype(BF16), v, preferred_element_type=F32)
                 + jnp.dot(qx, pre_ref[i], preferred_element_type=F32))
            yc = y - jnp.mean(y, axis=-1, keepdims=True)
            var = jnp.mean(yc * yc, axis=-1, keepdims=True)
            o_ref[0, rows, :] = (yc * lax.rsqrt(var + GN_EPS)).astype(BF16)
        return carry

    lax.fori_loop(0, n_chunks // group, increments, 0)
    state_ref[...] = jnp.zeros_like(state_ref)
    for j in range(n_chunks):
        scan(j, 0)
    lax.fori_loop(0, n_chunks // group, outputs, 0)


def _retention(q, k, v, decay_logit):
    bsz, t, _ = q.shape
    c = RET_BLOCK
    assert t % c == 0
    n_chunks = t // c
    qk_spec = pl.BlockSpec((1, t, RET_QK_DIM), lambda b, h: (b, 0, h))
    v_spec = pl.BlockSpec((1, t, RET_V_DIM), lambda b, h: (b, 0, h))
    return pl.pallas_call(
        functools.partial(_retention_kernel, group=_largest_divisor(n_chunks, RET_BLOCKS_PER_STEP)),
        grid=(bsz, RET_HEADS),
        in_specs=[pl.BlockSpec(memory_space=pltpu.SMEM), qk_spec, qk_spec, v_spec],
        out_specs=v_spec,
        out_shape=jax.ShapeDtypeStruct((bsz, t, RET_V_WIDTH), BF16),
        scratch_shapes=[pltpu.VMEM((n_chunks, 2 * RET_QK_DIM, RET_V_DIM), F32),
                        pltpu.VMEM((n_chunks, 2 * RET_QK_DIM, RET_V_DIM), BF16),
                        pltpu.VMEM((2, RET_QK_DIM, RET_V_DIM), F32),
                        pltpu.VMEM((c, c), F32),
                        pltpu.VMEM((2, c, RET_QK_DIM), F32),
                        pltpu.VMEM((2, c, RET_QK_DIM), F32)],
        compiler_params=_params("parallel", "parallel"),
        name="retention",
    )(decay_logit, q, k, v)


def _na_bias_table(rpb):
    qc = np.arange(GRID_W)[:, None]
    kc = np.arange(GRID_W)[None, :]
    cstart = np.clip(qc - NA_KW // 2, 0, GRID_W - NA_KW)
    valid = (kc >= cstart) & (kc < cstart + NA_KW)
    cidx = np.clip(kc - qc + NA_KW - 1, 0, 2 * NA_KW - 2)
    onehot = (cidx[None] == np.arange(2 * NA_KW - 1)[:, None, None]).astype(np.float32)
    tab = jnp.einsum("hrd,dqk->hrqk", rpb.astype(F32) * LOG2_E, onehot, precision=lax.Precision.HIGHEST)
    tab = jnp.where(valid, tab, MASK_VALUE)
    return jnp.concatenate([tab[:, :-1], tab[:, 1:]], axis=-1)


def _na_kernel(q_ref, k_ref, v_ref, bias_ref, o_ref, s_ref):
    w = GRID_W
    per = NA_ROWS_PER_STAGE
    n_pairs = q_ref.shape[1]
    n_rows = q_ref.shape[2] // w
    groups_per_pair = n_rows // per
    n_groups = n_pairs * groups_per_pair
    first_head = lax.broadcasted_iota(jnp.int32, (w, V7X_LANES), 1) < NA_HEAD_DIM

    def locate(g, i):
        g = jnp.int32(g)
        pair = lax.div(g, jnp.int32(groups_per_pair))
        r = (g - pair * groups_per_pair) * per + i
        start = jnp.clip(r - NA_KH // 2, 0, n_rows - NA_KH)
        return (pair, pl.ds(pl.multiple_of(r * w, w), w), start - r + NA_KH - 1,
                pl.ds(pl.multiple_of(start * w, w), NA_KH * w))

    def scores(g):
        for i in range(per):
            pair, rows, _, keys = locate(g, i)
            q = q_ref[0, pair, rows, :]
            zero = jnp.zeros_like(q)
            q2 = jnp.concatenate([jnp.where(first_head, q, zero), jnp.where(first_head, zero, q)], axis=0)
            s_ref[i] = lax.dot_general(q2, k_ref[0, pair, keys, :], (((1,), (1,)), ((), ())),
                                       preferred_element_type=F32)

    def attend(g):
        for i in range(per):
            pair, rows, off, keys = locate(g, i)
            probs, denoms = [], []
            for hh in range(NA_HEADS_PER_STEP):
                head = pair * NA_HEADS_PER_STEP + hh
                bias = jnp.concatenate([bias_ref[head, off + j] for j in range(0, NA_KH, 2)], axis=1)
                s = s_ref[i, hh * w:(hh + 1) * w, :] + bias
                p = jnp.exp2(s - jnp.max(s, axis=-1, keepdims=True))
                denoms.append(jnp.sum(p, axis=-1, keepdims=True))
                probs.append(p.astype(BF16))
            o = jnp.dot(jnp.concatenate(probs, axis=0), v_ref[0, pair, keys, :], preferred_element_type=F32)
            o = jnp.where(first_head, o[:w] / denoms[0], o[w:] / denoms[1])
            o_ref[0, pair, rows, :] = o.astype(BF16)

    def steady(g, carry):
        attend(g - 1)
        scores(g)
        return carry

    scores(0)
    lax.fori_loop(1, n_groups, steady, 0, unroll=NA_PIPELINE_UNROLL)
    attend(n_groups - 1)


def _neighborhood_attention(q, k, v, rpb):
    bsz, n_pairs, t, _ = q.shape
    n_rows = t // GRID_W
    assert t % GRID_W == 0 and n_rows >= NA_KH and n_rows % NA_ROWS_PER_STAGE == 0
    assert n_rows // NA_ROWS_PER_STAGE >= 2 and n_pairs * NA_HEADS_PER_STEP == NA_HEADS
    bias = _na_bias_table(rpb)
    spec = pl.BlockSpec((1, n_pairs, t, V7X_LANES), lambda b: (b, 0, 0, 0))
    stage_rows = NA_HEADS_PER_STEP * GRID_W
    return pl.pallas_call(
        _na_kernel,
        grid=(bsz,),
        in_specs=[spec, spec, spec, _resident(bias.shape)],
        out_specs=spec,
        out_shape=jax.ShapeDtypeStruct(q.shape, BF16),
        scratch_shapes=[pltpu.VMEM((NA_ROWS_PER_STAGE, stage_rows, NA_KH * GRID_W), F32)],
        compiler_params=_params("parallel"),
        name="neighborhood_attention",
    )(q, k, v, bias)


def _outproj_kernel(h_ref, g_ref, yn_ref, sg_ref, gain_ref, na_ref, gtr_ref, gtn_ref,
                    wr_ref, wn_ref, wo_ref, o_ref):
    slab = h_ref.shape[1] // OUTPROJ_ROW_SLABS
    for s in range(OUTPROJ_ROW_SLABS):
        rows = slice(s * slab, (s + 1) * slab)
        gated = sg_ref[0, rows, :].astype(F32) * (yn_ref[0, rows, :].astype(F32) * gain_ref[...])
        ret = jnp.dot(gated.astype(BF16), wr_ref[...], preferred_element_type=F32)
        na_in = jnp.concatenate([na_ref[0, p, rows, :] for p in range(na_ref.shape[1])], axis=1)
        na = jnp.dot(na_in, wn_ref[...], preferred_element_type=F32)
        merged = (jax.nn.sigmoid(gtr_ref[0, rows, :].astype(F32)) * ret
                  + jax.nn.sigmoid(gtn_ref[0, rows, :].astype(F32)) * na)
        out = jnp.dot(merged.astype(BF16), wo_ref[...], preferred_element_type=F32)
        o_ref[0, rows, :] = h_ref[0, rows, :] + g_ref[0, 0] * out


def _outproj(h, mod, yn, silu_g, gn_gain, na, gate_r, gate_n, w_ret_o, w_na_o, w_out):
    bsz, t, d = h.shape
    tm = min(TOKEN_TILE, t)
    assert t % tm == 0 and tm % OUTPROJ_ROW_SLABS == 0

    def tile(width):
        return pl.BlockSpec((1, tm, width), lambda b, i: (b, i, 0))

    return pl.pallas_call(
        _outproj_kernel,
        grid=(bsz, t // tm),
        in_specs=[tile(d), _mod_spec(5), tile(RET_V_WIDTH), tile(RET_V_WIDTH), _resident((1, RET_V_WIDTH)),
                  pl.BlockSpec((1, na.shape[1], tm, V7X_LANES), lambda b, i: (b, 0, i, 0)), tile(d), tile(d),
                  _resident(w_ret_o.shape), _resident(w_na_o.shape), _resident(w_out.shape)],
        out_specs=tile(d),
        out_shape=jax.ShapeDtypeStruct(h.shape, F32),
        compiler_params=_params("parallel", "parallel"),
        name="mixer_outproj",
    )(h, mod, yn, silu_g, gn_gain.reshape(1, RET_V_WIDTH), na, gate_r, gate_n, w_ret_o, w_na_o, w_out)


def _trunk(x, mods, weights, final_norm):
    assert x.shape[1] % RET_CHUNK == 0 and x.shape[2] == D_MODEL
    h = x
    depth = len(weights)
    for l, wl in enumerate(weights):
        (ffn1_norm, ffn1_wi, ffn1_wo, mix_norm, w_in, decay_logit, gn_gain, w_ret_o, rpb,
         w_na_o, w_out, ffn2_norm, ffn2_wi, ffn2_wo) = wl
        mod = mods[l]
        h = _ffn(h, mod, 0, ffn1_norm, ffn1_wi, ffn1_wo)
        q_r, k_r, v_r, silu_g, q_n, k_n, v_n, gate_r, gate_n = _inproj(h, mod, mix_norm, w_in)
        yn = _retention(q_r, k_r, v_r, decay_logit)
        na = _neighborhood_attention(q_n, k_n, v_n, rpb)
        h = _outproj(h, mod, yn, silu_g, gn_gain, na, gate_r, gate_n, w_ret_o, w_na_o, w_out)
        h = _ffn(h, mod, 6, ffn2_norm, ffn2_wi, ffn2_wo,
                 final_gain=final_norm if l == depth - 1 else None)
    return h


def kernel(x_prompt, x_sample, c_prompt, c_sample, ada_w, ada_b, ffn1_norm, ffn1_wi, ffn1_wo, mix_norm, w_in, ret_decay_logit, ret_gn_gain, w_ret_o, na_rpb, w_na_o, w_out, ffn2_norm, ffn2_wi, ffn2_wo, final_norm):
    depth = ada_w.shape[0]
    nbp = c_prompt.shape[0]
    mods = _adaln(jnp.concatenate([c_prompt, c_sample], axis=0), ada_w, ada_b)
    mods = mods.reshape(depth, mods.shape[1], N_MOD, 1, D_MODEL)
    bf = lambda w: w.astype(BF16)
    weights = [
        (ffn1_norm[l], bf(ffn1_wi[l]), bf(ffn1_wo[l]), mix_norm[l], bf(w_in[l]), ret_decay_logit[l],
         ret_gn_gain[l], bf(w_ret_o[l]), na_rpb[l], bf(w_na_o[l]), bf(w_out[l]),
         ffn2_norm[l], bf(ffn2_wi[l]), bf(ffn2_wo[l]))
        for l in range(depth)
    ]
    y_prompt = _trunk(x_prompt, mods[:, :nbp], weights, final_norm)
    y_sample = _trunk(x_sample, mods[:, nbp:], weights, final_norm)
    return (y_prompt, y_sample)
```
